```python
import jax, jax.numpy as jnp
from jax import lax
import numpy as np

D_MODEL = 2048
BATCH = 1
SEQ = 16384
DEPTH = 1

CHUNK = 64
D_MIX = D_MODEL
D_HGRN = D_MIX // 2
HGRN_HEAD_DIM = 128
HGRN_HEADS = D_HGRN // HGRN_HEAD_DIM
D_SSM = D_MIX - D_HGRN
SSM_HEAD_DIM = 64
SSM_HEADS = D_SSM // SSM_HEAD_DIM
SSM_GROUPS = 4
SSM_HPG = SSM_HEADS // SSM_GROUPS
SSM_STATE = 128
SSM_CONV = 4
SSM_CONV_DIM = D_SSM + 2 * SSM_GROUPS * SSM_STATE
D_IN_PROJ = 4 * D_HGRN + D_SSM + SSM_CONV_DIM + SSM_HEADS
D_FF = 5632
FFN_CONV = 3
N_MOD = 6
EPS = 1e-6
DT_MIN = 1e-3
DT_MAX = 1e-1
A_MIN = 1.0
A_MAX = 16.0

kernel_name = 'hybrid_hgrn2_ssd_convffn_adaln'


def rms_norm(x, w):
    xf = x.astype(jnp.float32)
    y = xf * lax.rsqrt(jnp.mean(xf * xf, axis=-1, keepdims=True) + EPS)
    return (y * w.astype(jnp.float32)).astype(x.dtype)


def causal_dwconv(x, w, b):
    width = w.shape[0]
    y = lax.conv_general_dilated(
        x, w[:, None, :].astype(x.dtype), window_strides=(1,), padding=[(width - 1, 0)],
        dimension_numbers=('NWC', 'WIO', 'NWC'), feature_group_count=x.shape[-1])
    return y + b.astype(x.dtype)


def hgrn2_mixer(q, f_pre, v, g, lb, gnorm_w):
    bsz, seq, _ = q.shape
    H, K, C = HGRN_HEADS, HGRN_HEAD_DIM, CHUNK
    nc = seq // C
    f32 = jnp.float32
    qf = jax.nn.silu(q.astype(f32)).reshape(bsz, seq, H, K)
    ff = f_pre.astype(f32).reshape(bsz, seq, H, K)
    vf = v.astype(f32).reshape(bsz, seq, H, K)
    lbh = lb.astype(f32).reshape(H, K)
    log_f = jnp.log(lbh + (1.0 - lbh) * jax.nn.sigmoid(ff))
    kf = (1.0 - lbh) * jax.nn.sigmoid(-ff)

    def to_chunks(t):
        return t.reshape(bsz, nc, C, H, K).transpose(1, 0, 3, 2, 4)

    qc, kc, vc, lc = to_chunks(qf), to_chunks(kf), to_chunks(vf), to_chunks(log_f)
    b = jnp.cumsum(lc, axis=3)
    b_last = b[:, :, :, -1:, :]
    q_dec = qc * jnp.exp(b)
    k_dec = kc * jnp.exp(b_last - b)
    chunk_decay = jnp.exp(b_last[:, :, :, 0, :])
    mask = jnp.tril(jnp.ones((C, C), dtype=bool))[:, :, None]

    def step(state, inp):
        qq, kk, vv, bb, qd, kd, dc = inp
        diff = bb[:, :, :, None, :] - bb[:, :, None, :, :]
        w = jnp.exp(jnp.where(mask, diff, -jnp.inf))
        att = jnp.einsum('bhtk,bhsk,bhtsk->bhts', qq, kk, w)
        out = jnp.einsum('bhts,bhsv->bhtv', att, vv) + jnp.einsum('bhtk,bhkv->bhtv', qd, state)
        state = dc[..., None] * state + jnp.einsum('bhsk,bhsv->bhkv', kd, vv)
        return state, out

    s0 = jnp.zeros((bsz, H, K, K), f32)
    _, o = lax.scan(step, s0, (qc, kc, vc, b, q_dec, k_dec, chunk_decay))
    o = o.transpose(1, 0, 3, 2, 4).reshape(bsz, seq, H, K)
    o = o * lax.rsqrt(jnp.mean(o * o, axis=-1, keepdims=True) + EPS)
    o = o.reshape(bsz, seq, D_HGRN) * gnorm_w.astype(f32) * jax.nn.silu(g.astype(f32))
    return o.astype(q.dtype)


def ssd_mixer(z, xbc, dt_raw, conv_w, conv_b, dt_bias, a_log, d_skip, norm_w):
    bsz, seq, _ = z.shape
    G, R, P, N, C = SSM_GROUPS, SSM_HPG, SSM_HEAD_DIM, SSM_STATE, CHUNK
    nc = seq // C
    f32 = jnp.float32
    xbc = jax.nn.silu(causal_dwconv(xbc, conv_w, conv_b)).astype(f32)
    xs, bm, cm = jnp.split(xbc, [D_SSM, D_SSM + G * N], axis=-1)
    x = xs.reshape(bsz, nc, C, G, R, P)
    bm = bm.reshape(bsz, nc, C, G, N)
    cm = cm.reshape(bsz, nc, C, G, N)
    dt = jax.nn.softplus(dt_raw.astype(f32) + dt_bias.astype(f32)).reshape(bsz, nc, C, G, R)
    a_head = -jnp.exp(a_log.astype(f32)).reshape(G, R)
    acum = jnp.cumsum((dt * a_head).transpose(0, 1, 3, 4, 2), axis=-1)
    xdt = x * dt[..., None]
    mask = jnp.tril(jnp.ones((C, C), dtype=bool))
    seg = jnp.exp(jnp.where(mask, acum[..., :, None] - acum[..., None, :], -jnp.inf))
    cb = jnp.einsum('bctgn,bcsgn->bcgts', cm, bm)
    y_diag = jnp.einsum('bcgts,bcgrts,bcsgrp->bctgrp', cb, seg, xdt)
    decay_states = jnp.exp(acum[..., -1:] - acum)
    states = jnp.einsum('bcsgn,bcgrs,bcsgrp->bcgrpn', bm, decay_states, xdt)
    chunk_decay = jnp.exp(acum[..., -1])

    def step(h, inp):
        st, dc = inp
        return dc[..., None, None] * h + st, h

    h0 = jnp.zeros((bsz, G, R, P, N), f32)
    _, h_prev = lax.scan(step, h0, (states.transpose(1, 0, 2, 3, 4, 5), chunk_decay.transpose(1, 0, 2, 3)))
    h_prev = h_prev.transpose(1, 0, 2, 3, 4, 5)
    y_off = jnp.einsum('bctgn,bcgrpn,bcgrt->bctgrp', cm, h_prev, jnp.exp(acum))
    y = y_diag + y_off + d_skip.astype(f32).reshape(G, R)[:, :, None] * x
    y = y.reshape(bsz, seq, D_SSM) * jax.nn.silu(z.astype(f32))
    yg = y.reshape(bsz, seq, G, D_SSM // G)
    yg = yg * lax.rsqrt(jnp.mean(yg * yg, axis=-1, keepdims=True) + EPS)
    y = yg.reshape(bsz, seq, D_SSM) * norm_w.astype(f32)
    return y.astype(z.dtype)


def setup_inputs(seed: int = 0) -> dict:
    key = jax.random.key(seed)
    ks = jax.random.split(key, 24)

    def nrm(k, shape, scale):
        return jax.random.normal(k, shape, jnp.float32) * scale

    dt0 = jnp.exp(jax.random.uniform(ks[10], (DEPTH, SSM_HEADS), jnp.float32,
                                     np.log(DT_MIN), np.log(DT_MAX)))
    return {
        'x': nrm(ks[0], (BATCH, SEQ, D_MODEL), 1.0),
        'c': nrm(ks[1], (BATCH, D_MODEL), 1.0),
        'w_mod': nrm(ks[2], (DEPTH, D_MODEL, N_MOD * D_MODEL), 0.5 * D_MODEL ** -0.5),
        'b_mod': nrm(ks[3], (DEPTH, N_MOD * D_MODEL), 0.02),
        'norm1_w': 1.0 + nrm(ks[4], (DEPTH, D_MODEL), 0.02),
        'w_in': nrm(ks[5], (DEPTH, D_MODEL, D_IN_PROJ), D_MODEL ** -0.5),
        'hgrn_lb': nrm(ks[6], (DEPTH + 1, D_HGRN), 1.0),
        'hgrn_gnorm_w': 1.0 + nrm(ks[7], (DEPTH, D_HGRN), 0.02),
        'ssd_conv_w': nrm(ks[8], (DEPTH, SSM_CONV, SSM_CONV_DIM), SSM_CONV ** -0.5),
        'ssd_conv_b': nrm(ks[9], (DEPTH, SSM_CONV_DIM), 0.02),
        'ssd_dt_bias': dt0 + jnp.log(-jnp.expm1(-dt0)),
        'ssd_a_log': jnp.log(jax.random.uniform(ks[11], (DEPTH, SSM_HEADS), jnp.float32, A_MIN, A_MAX)),
        'ssd_d': 1.0 + nrm(ks[12], (DEPTH, SSM_HEADS), 0.1),
        'ssd_norm_w': 1.0 + nrm(ks[13], (DEPTH, D_SSM), 0.02),
        'w_out': nrm(ks[14], (DEPTH, D_MIX, D_MODEL), D_MIX ** -0.5),
        'norm2_w': 1.0 + nrm(ks[15], (DEPTH, D_MODEL), 0.02),
        'ffn_w_up': nrm(ks[16], (DEPTH, D_MODEL, 2 * D_FF), D_MODEL ** -0.5),
        'ffn_conv_w': nrm(ks[17], (DEPTH, FFN_CONV, 2 * D_FF), FFN_CONV ** -0.5),
        'ffn_conv_b': nrm(ks[18], (DEPTH, 2 * D_FF), 0.02),
        'ffn_w_down': nrm(ks[19], (DEPTH, D_FF, D_MODEL), D_FF ** -0.5),
        'final_norm_w': 1.0 + nrm(ks[20], (D_MODEL,), 0.02),
    }


def reference(x, c, w_mod, b_mod, norm1_w, w_in, hgrn_lb, hgrn_gnorm_w, ssd_conv_w, ssd_conv_b,
              ssd_dt_bias, ssd_a_log, ssd_d, ssd_norm_w, w_out, norm2_w, ffn_w_up, ffn_conv_w,
              ffn_conv_b, ffn_w_down, final_norm_w):
    split_idx = [D_HGRN, 2 * D_HGRN, 3 * D_HGRN, 4 * D_HGRN, 4 * D_HGRN + D_SSM,
                 4 * D_HGRN + D_SSM + SSM_CONV_DIM]
    lower_bounds = jnp.cumsum(jax.nn.softmax(hgrn_lb.astype(jnp.float32), axis=0), axis=0)
    for l in range(DEPTH):
        mod = (jax.nn.silu(c) @ w_mod[l] + b_mod[l])[:, None, :]
        shift1, scale1, gate1, shift2, scale2, gate2 = jnp.split(mod, N_MOD, axis=-1)

        h = rms_norm(x, norm1_w[l]) * (1.0 + scale1) + shift1
        proj = h @ w_in[l]
        q_h, f_h, i_h, g_h, z_s, xbc_s, dt_s = jnp.split(proj, split_idx, axis=-1)
        o_a = hgrn2_mixer(q_h, f_h, i_h, g_h, lower_bounds[l], hgrn_gnorm_w[l])
        o_b = ssd_mixer(z_s, xbc_s, dt_s, ssd_conv_w[l], ssd_conv_b[l], ssd_dt_bias[l],
                        ssd_a_log[l], ssd_d[l], ssd_norm_w[l])
        mixed = jnp.concatenate([o_a, o_b], axis=-1) @ w_out[l]
        x = x + gate1 * mixed

        h = rms_norm(x, norm2_w[l]) * (1.0 + scale2) + shift2
        u = causal_dwconv(h @ ffn_w_up[l], ffn_conv_w[l], ffn_conv_b[l])
        u_gate, u_val = jnp.split(u, 2, axis=-1)
        x = x + gate2 * ((jax.nn.silu(u_gate) * u_val) @ ffn_w_down[l])
    return rms_norm(x, final_norm_w)
```

```python
import functools

import jax
import jax.numpy as jnp
from jax import lax
from jax.experimental import pallas as pl
from jax.experimental.pallas import tpu as pltpu

F32 = jnp.float32
BF16 = jnp.bfloat16

D_MODEL = 2048
D_HGRN = 1024
HGRN_HEADS = 8
HEAD_K = 128
D_SSM = 1024
SSM_HEADS = 16
SSM_P = 64
SSM_GROUPS = 4
SSM_N = 128
SSM_CONV = 4
D_MAIN = 4 * D_HGRN + D_SSM + (D_SSM + 2 * SSM_GROUPS * SSM_N)
D_FF = 5632
FFN_CONV = 3
N_MOD = 6
EPS = 1e-6

LANES = 128
SUBLANES = 8
CHUNK = 128
HALO = SUBLANES
VMEM_LIMIT = 56 * 1024 * 1024


def _cparams(n_axes):
    return pltpu.CompilerParams(dimension_semantics=("arbitrary",) * n_axes,
                                vmem_limit_bytes=VMEM_LIMIT)


def _silu(x):
    return x * jax.nn.sigmoid(x)


def _split3(x):
    hi = x.astype(BF16)
    r1 = x - hi.astype(F32)
    mid = r1.astype(BF16)
    r2 = r1 - mid.astype(F32)
    return hi, mid, r2.astype(BF16)


def _cumsum_rows(tri3, x):
    hi, mid, lo = _split3(x)
    return jnp.dot(tri3, jnp.concatenate([hi, mid, lo], axis=0), preferred_element_type=F32)


def _mod_kernel(c_ref, w_ref, b_ref, o_ref):
    s = _silu(c_ref[...])
    o_ref[...] = jnp.sum(w_ref[...] * s, axis=0, keepdims=True) + b_ref[...]


def _mod_call(c_col, w_mod, b_mod):
    d, n = w_mod.shape
    tn = 1024
    return pl.pallas_call(
        _mod_kernel,
        grid=(n // tn,),
        in_specs=[pl.BlockSpec((d, 1), lambda j: (0, 0)),
                  pl.BlockSpec((d, tn), lambda j: (0, j)),
                  pl.BlockSpec((1, tn), lambda j: (0, j))],
        out_specs=pl.BlockSpec((1, tn), lambda j: (0, j)),
        out_shape=jax.ShapeDtypeStruct((1, n), F32),
        compiler_params=_cparams(1),
        name="mod",
    )(c_col, w_mod, b_mod)


def _norm_mod_rows(x_ref, nw_ref, sc_ref, sh_ref, h_ref, n_rows, row_block=32):
    nw = nw_ref[...]
    sc = 1.0 + sc_ref[...]
    sh = sh_ref[...]

    def body(r, carry):
        rows = pl.ds(pl.multiple_of(r * row_block, row_block), row_block)
        x = x_ref[rows, :]
        y = x * lax.rsqrt(jnp.mean(x * x, axis=-1, keepdims=True) + EPS)
        h_ref[rows, :] = ((y * nw) * sc + sh).astype(BF16)
        return carry

    lax.fori_loop(0, n_rows // row_block, body, 0)


def _inproj_kernel(x_ref, nw_ref, sc_ref, sh_ref, w_ref, wdt_ref, o_ref, dt_ref, h_ref, *, tm):
    @pl.when(pl.program_id(1) == 0)
    def _():
        _norm_mod_rows(x_ref, nw_ref, sc_ref, sh_ref, h_ref, tm)
        dt_ref[...] = jnp.dot(h_ref[...], wdt_ref[...], preferred_element_type=F32)

    o_ref[...] = jnp.dot(h_ref[...], w_ref[...].astype(BF16), preferred_element_type=F32)


def _inproj_call(x2, norm_w, mod, w_in, w_dt, tm, tn):
    s = x2.shape[0]
    return pl.pallas_call(
        functools.partial(_inproj_kernel, tm=tm),
        grid=(s // tm, D_MAIN // tn),
        in_specs=[pl.BlockSpec((tm, D_MODEL), lambda i, j: (i, 0)),
                  pl.BlockSpec((1, D_MODEL), lambda i, j: (0, 0)),
                  pl.BlockSpec((1, D_MODEL), lambda i, j: (0, 1)),
                  pl.BlockSpec((1, D_MODEL), lambda i, j: (0, 0)),
                  pl.BlockSpec((D_MODEL, tn), lambda i, j: (0, j)),
                  pl.BlockSpec((D_MODEL, LANES), lambda i, j: (0, 0))],
        out_specs=[pl.BlockSpec((tm, tn), lambda i, j: (i, j)),
                   pl.BlockSpec((tm, LANES), lambda i, j: (i, 0))],
        out_shape=[jax.ShapeDtypeStruct((s, D_MAIN), F32),
                   jax.ShapeDtypeStruct((s, LANES), F32)],
        scratch_shapes=[pltpu.VMEM((tm, D_MODEL), BF16)],
        compiler_params=_cparams(2),
        name="inproj",
    )(x2, norm_w, mod, mod, w_in, w_dt)


def _bcast_rows(b_ref, h, anchors, rows_each):
    parts = [jnp.broadcast_to(b_ref[h, a:a + 1, :], (rows_each, LANES)) for a in anchors]
    return parts[0] if len(parts) == 1 else jnp.concatenate(parts, axis=0)


def _hgrn_kernel(q_ref, f_ref, v_ref, g_ref, lb_ref, gnw_ref, tri_ref, o_ref, st_ref, b_ref, *, n_chunks):
    c_len = CHUNK
    n_levels = c_len.bit_length() - 1

    @pl.when(pl.program_id(0) == 0)
    def _():
        st_ref[...] = jnp.zeros_like(st_ref)

    row2 = lax.broadcasted_iota(jnp.int32, (c_len, c_len), 0)
    col2 = lax.broadcasted_iota(jnp.int32, (c_len, c_len), 1)
    xor2 = row2 ^ col2
    below = row2 > col2
    rowk = lax.broadcasted_iota(jnp.int32, (c_len, LANES), 0)
    sub8 = lax.broadcasted_iota(jnp.int32, (SUBLANES, LANES), 0)
    tri3 = tri_ref[...]

    def chunk(c, carry):
        rows = pl.ds(pl.multiple_of(c * c_len, c_len), c_len)
        for h in range(HGRN_HEADS):
            cs = slice(h * HEAD_K, (h + 1) * HEAD_K)
            a0 = lb_ref[0:1, cs]
            a1 = lb_ref[1:2, cs]
            amax = jnp.maximum(a0, a1)
            e0 = jnp.exp(a0 - amax)
            lb = e0 / (e0 + jnp.exp(a1 - amax))
            q = q_ref[rows, cs]
            ff = f_ref[rows, cs]
            v = v_ref[rows, cs]
            qs = _silu(q)
            fg = lb + (1.0 - lb) * jax.nn.sigmoid(ff)
            lf = jnp.log(fg)
            k = (1.0 - lb) * jax.nn.sigmoid(-ff)
            b = _cumsum_rows(tri3, lf)
            b_ref[h] = b
            vb = v.astype(BF16)

            att = jnp.where(row2 == col2,
                            lax.dot_general(qs.astype(BF16), k.astype(BF16), (((1,), (1,)), ((), ())),
                                            preferred_element_type=F32), 0.0)
            for lvl in range(n_levels):
                m = 1 << lvl
                upper = ((rowk >> lvl) & 1) == 1
                if lvl == 0:
                    xl = jnp.where(upper, qs * fg, k)
                else:
                    anchors = [j * 2 * m + m - 1 for j in range(c_len // (2 * m))]
                    if m >= SUBLANES:
                        bc = _bcast_rows(b_ref, h, anchors, 2 * m)
                    elif 2 * m == SUBLANES:
                        bc = _bcast_rows(b_ref, h, anchors, SUBLANES)
                    else:
                        tiles = []
                        for t8 in range(c_len // SUBLANES):
                            lo = jnp.broadcast_to(b_ref[h, t8 * 8 + 1:t8 * 8 + 2, :], (SUBLANES, LANES))
                            hi = jnp.broadcast_to(b_ref[h, t8 * 8 + 5:t8 * 8 + 6, :], (SUBLANES, LANES))
                            tiles.append(jnp.where(sub8 < 4, lo, hi))
                        bc = jnp.concatenate(tiles, axis=0)
                    d = b - bc
                    e = jnp.exp(jnp.where(upper, d, -d))
                    xl = jnp.where(upper, qs, k) * e
                xb = xl.astype(BF16)
                p = lax.dot_general(xb, xb, (((1,), (1,)), ((), ())), preferred_element_type=F32)
                att = att + jnp.where(((xor2 >> lvl) == 1) & below, p, 0.0)

            st = st_ref[h]
            qc = (qs * jnp.exp(b)).astype(BF16)
            o = jnp.dot(att.astype(BF16), vb, preferred_element_type=F32)
            o = o + lax.dot_general(qc, st.astype(BF16), (((1,), (1,)), ((), ())),
                                    preferred_element_type=F32)
            b_last = b_ref[h, c_len - 1:c_len, :]
            ke = (k * jnp.exp(b_last - b)).astype(BF16)
            vt = v.T.astype(BF16)
            st_ref[h] = st * jnp.exp(b_last) + jnp.dot(vt, ke, preferred_element_type=F32)

            on = o * lax.rsqrt(jnp.mean(o * o, axis=-1, keepdims=True) + EPS)
            o_ref[rows, cs] = (on * gnw_ref[0:1, cs] * _silu(g_ref[rows, cs])).astype(BF16)
        return carry

    lax.fori_loop(0, n_chunks, chunk, 0)


def _hgrn_call(proj, hgrn_lb, gnorm_w, tri3, tile):
    s = proj.shape[0]
    def col(k):
        return pl.BlockSpec((tile, D_HGRN), lambda i: (i, k))

    return pl.pallas_call(
        functools.partial(_hgrn_kernel, n_chunks=tile // CHUNK),
        grid=(s // tile,),
        in_specs=[col(0), col(1), col(2), col(3),
                  pl.BlockSpec((2, D_HGRN), lambda i: (0, 0)),
                  pl.BlockSpec((1, D_HGRN), lambda i: (0, 0)),
                  pl.BlockSpec((CHUNK, 3 * CHUNK), lambda i: (0, 0))],
        out_specs=pl.BlockSpec((tile, D_HGRN), lambda i: (i, 0)),
        out_shape=jax.ShapeDtypeStruct((s, D_HGRN), BF16),
        scratch_shapes=[pltpu.VMEM((HGRN_HEADS, HEAD_K, HEAD_K), F32),
                        pltpu.VMEM((HGRN_HEADS, CHUNK, LANES), F32)],
        compiler_params=_cparams(1),
        name="hgrn",
    )(proj, proj, proj, proj, hgrn_lb, gnorm_w, tri3)


def _ssd_kernel(z_ref, xr_ref, bcr_ref, dt_ref, cwx_ref, cwbc_ref, cbx_ref, cbbc_ref, dtb_ref, alog_ref,
                dexp_ref, nw_ref, tri_ref, o_ref,
                ht_ref, halo_x, halo_bc, ext_x, ext_bc, cx_ref, cbc_ref,
                acum_s, ats_s, wdec_s, eac_s, y_s, *, tile):
    c_len = CHUNK
    n_chunks = tile // c_len
    pair_w = 2 * SSM_P

    @pl.when(pl.program_id(0) == 0)
    def _():
        ht_ref[...] = jnp.zeros_like(ht_ref)
        halo_x[...] = jnp.zeros_like(halo_x)
        halo_bc[...] = jnp.zeros_like(halo_bc)

    for raw_ref, halo, ext, cw_ref, cb_ref, out in ((xr_ref, halo_x, ext_x, cwx_ref, cbx_ref, cx_ref),
                                                    (bcr_ref, halo_bc, ext_bc, cwbc_ref, cbbc_ref, cbc_ref)):
        ext[0:HALO, :] = halo[...]
        ext[HALO:HALO + tile, :] = raw_ref[...]
        halo[...] = raw_ref[tile - HALO:tile, :]
        rb = 64
        for r in range(tile // rb):
            acc = cb_ref[...] + cw_ref[SSM_CONV - 1:SSM_CONV, :] * ext[HALO + r * rb:HALO + (r + 1) * rb, :]
            for j in range(SSM_CONV - 1):
                off = HALO + r * rb - (SSM_CONV - 1) + j
                acc = acc + cw_ref[j:j + 1, :] * ext[off:off + rb, :]
            out[r * rb:(r + 1) * rb, :] = _silu(acc)

    lane = lax.broadcasted_iota(jnp.int32, (c_len, LANES), 1)
    row2 = lax.broadcasted_iota(jnp.int32, (c_len, c_len), 0)
    col2 = lax.broadcasted_iota(jnp.int32, (c_len, c_len), 1)
    causal = row2 >= col2
    first_head = lane < SSM_P
    head_lane = lax.broadcasted_iota(jnp.int32, (1, LANES), 1) < SSM_HEADS
    a_head = jnp.where(head_lane, -jnp.exp(alog_ref[...]), 0.0)
    tri3 = tri_ref[...]

    def chunk(c, carry):
        rows = pl.ds(pl.multiple_of(c * c_len, c_len), c_len)
        dt = jax.nn.softplus(dt_ref[rows, :] + dtb_ref[...])
        acum = _cumsum_rows(tri3, dt * a_head)
        a_last = acum[c_len - 1:c_len, :]
        acum_s[...] = acum
        ats_s[...] = (acum - jnp.log(dt)).T
        wdec_s[...] = dt * jnp.exp(a_last - acum)
        eac_s[...] = jnp.exp(acum)

        for g in range(SSM_GROUPS):
            bg = cbc_ref[rows, g * SSM_N:(g + 1) * SSM_N]
            cg = cbc_ref[rows, (SSM_GROUPS + g) * SSM_N:(SSM_GROUPS + g + 1) * SSM_N]
            cb = lax.dot_general(cg.astype(BF16), bg.astype(BF16), (((1,), (1,)), ((), ())),
                                 preferred_element_type=F32)
            bgt = bg.T.astype(BF16)
            for pp in range(2):
                pair = 2 * g + pp
                ha, hb = 2 * pair, 2 * pair + 1
                ps = slice(pair * pair_w, (pair + 1) * pair_w)
                xp = cx_ref[rows, ps]
                xpb = xp.astype(BF16)
                zero = jnp.zeros_like(xpb)
                y = None
                for hh, keep in ((ha, first_head), (hb, ~first_head)):
                    ex = acum_s[:, hh:hh + 1] - ats_s[hh:hh + 1, :]
                    m = (cb * jnp.exp(jnp.where(causal, ex, -jnp.inf))).astype(BF16)
                    yh = jnp.dot(m, jnp.where(keep, xpb, zero), preferred_element_type=F32)
                    y = yh if y is None else y + yh
                ce = jnp.concatenate([(cg * eac_s[:, ha:ha + 1]).astype(BF16),
                                      (cg * eac_s[:, hb:hb + 1]).astype(BF16)], axis=1)
                ht = ht_ref[pair]
                y = y + jnp.dot(ce, ht.astype(BF16), preferred_element_type=F32)

                xw = (xp * jnp.where(first_head, wdec_s[:, ha:ha + 1], wdec_s[:, hb:hb + 1])).astype(BF16)
                dca = jnp.exp(acum_s[c_len - 1:c_len, ha:ha + 1])
                dcb = jnp.exp(acum_s[c_len - 1:c_len, hb:hb + 1])
                ht_ref[pair, 0:SSM_N, :] = ht[0:SSM_N] * dca + jnp.dot(
                    bgt, jnp.where(first_head, xw, zero), preferred_element_type=F32)
                ht_ref[pair, SSM_N:2 * SSM_N, :] = ht[SSM_N:] * dcb + jnp.dot(
                    bgt, jnp.where(first_head, zero, xw), preferred_element_type=F32)

                y = y + dexp_ref[0:1, ps] * xp
                y_s[:, ps] = y * _silu(z_ref[rows, ps])

            gs = slice(g * 2 * pair_w, (g + 1) * 2 * pair_w)
            yg = y_s[:, gs]
            yn = yg * lax.rsqrt(jnp.mean(yg * yg, axis=-1, keepdims=True) + EPS)
            o_ref[rows, gs] = (yn * nw_ref[0:1, gs]).astype(BF16)
        return carry

    lax.fori_loop(0, n_chunks, chunk, 0)


def _ssd_call(proj, dt_raw, conv_w, conv_b, dt_bias, a_log, d_exp, norm_w, tri3, tile):
    s = proj.shape[0]
    def col(k):
        return pl.BlockSpec((tile, D_SSM), lambda i: (i, k))

    def full(shape):
        return pl.BlockSpec(shape, lambda i: (0, 0))

    n_pairs = SSM_HEADS // 2
    return pl.pallas_call(
        functools.partial(_ssd_kernel, tile=tile),
        grid=(s // tile,),
        in_specs=[col(4), col(5), col(6),
                  pl.BlockSpec((tile, LANES), lambda i: (i, 0)),
                  pl.BlockSpec((SSM_CONV, D_SSM), lambda i: (0, 0)),
                  pl.BlockSpec((SSM_CONV, D_SSM), lambda i: (0, 1)),
                  pl.BlockSpec((1, D_SSM), lambda i: (0, 0)),
                  pl.BlockSpec((1, D_SSM), lambda i: (0, 1)),
                  full((1, LANES)), full((1, LANES)), full((1, D_SSM)), full((1, D_SSM)),
                  full((CHUNK, 3 * CHUNK))],
        out_specs=pl.BlockSpec((tile, D_SSM), lambda i: (i, 0)),
        out_shape=jax.ShapeDtypeStruct((s, D_SSM), BF16),
        scratch_shapes=[pltpu.VMEM((n_pairs, 2 * SSM_N, LANES), F32),
                        pltpu.VMEM((HALO, D_SSM), F32), pltpu.VMEM((HALO, D_SSM), F32),
                        pltpu.VMEM((tile + HALO, D_SSM), F32), pltpu.VMEM((tile + HALO, D_SSM), F32),
                        pltpu.VMEM((tile, D_SSM), F32), pltpu.VMEM((tile, D_SSM), F32),
                        pltpu.VMEM((CHUNK, LANES), F32), pltpu.VMEM((CHUNK, LANES), F32),
                        pltpu.VMEM((CHUNK, LANES), F32), pltpu.VMEM((CHUNK, LANES), F32),
                        pltpu.VMEM((CHUNK, D_SSM), F32)],
        compiler_params=_cparams(1),
        name="ssd",
    )(proj, proj, proj, dt_raw, conv_w, conv_w, conv_b, conv_b, dt_bias, a_log, d_exp, norm_w, tri3)


def _outproj_kernel(x_ref, oa_ref, ob_ref, wa_ref, wb_ref, g_ref, o_ref):
    mixed = jnp.dot(oa_ref[...], wa_ref[...], preferred_element_type=F32)
    mixed = mixed + jnp.dot(ob_ref[...], wb_ref[...], preferred_element_type=F32)
    o_ref[...] = x_ref[...] + g_ref[...] * mixed


def _outproj_call(x2, o_a, o_b, w_out_bf, mod, tm):
    s = x2.shape[0]
    return pl.pallas_call(
        _outproj_kernel,
        grid=(s // tm,),
        in_specs=[pl.BlockSpec((tm, D_MODEL), lambda i: (i, 0)),
                  pl.BlockSpec((tm, D_HGRN), lambda i: (i, 0)),
                  pl.BlockSpec((tm, D_SSM), lambda i: (i, 0)),
                  pl.BlockSpec((D_HGRN, D_MODEL), lambda i: (0, 0)),
                  pl.BlockSpec((D_SSM, D_MODEL), lambda i: (1, 0)),
                  pl.BlockSpec((1, D_MODEL), lambda i: (0, 2))],
        out_specs=pl.BlockSpec((tm, D_MODEL), lambda i: (i, 0)),
        out_shape=jax.ShapeDtypeStruct((s, D_MODEL), F32),
        compiler_params=_cparams(1),
        name="outproj",
    )(x2, o_a, o_b, w_out_bf, w_out_bf, mod)


def _ffn_kernel(x_ref, nw_ref, sc_ref, sh_ref, gate_ref, fnw_ref, wug_ref, wuv_ref, cwg_ref, cwv_ref,
                cbg_ref, cbv_ref, wd_ref, o_ref, h_ref, acc_ref, ug_ref, uv_ref, carry_ref, *, tm, n_slabs):
    i = pl.program_id(0)
    j = pl.program_id(1)

    @pl.when(j == 0)
    def _():
        _norm_mod_rows(x_ref, nw_ref, sc_ref, sh_ref, h_ref, tm)

    @pl.when(i == 0)
    def _():
        carry_ref[j] = jnp.zeros(carry_ref.shape[1:], F32)

    h = h_ref[...]
    ug_ref[0:HALO, :] = carry_ref[j, 0]
    uv_ref[0:HALO, :] = carry_ref[j, 1]
    ug_ref[HALO:HALO + tm, :] = jnp.dot(h, wug_ref[...], preferred_element_type=F32)
    uv_ref[HALO:HALO + tm, :] = jnp.dot(h, wuv_ref[...], preferred_element_type=F32)
    carry_ref[j, 0] = ug_ref[tm:tm + HALO, :]
    carry_ref[j, 1] = uv_ref[tm:tm + HALO, :]

    rb = 128
    for r in range(tm // rb):
        def conv(u_ref, cw_ref, cb_ref):
            acc = cb_ref[...] + cw_ref[FFN_CONV - 1:FFN_CONV, :] * u_ref[HALO + r * rb:HALO + (r + 1) * rb, :]
            for t in range(FFN_CONV - 1):
                off = HALO + r * rb - (FFN_CONV - 1) + t
                acc = acc + cw_ref[t:t + 1, :] * u_ref[off:off + rb, :]
            return acc
        act = (_silu(conv(ug_ref, cwg_ref, cbg_ref)) * conv(uv_ref, cwv_ref, cbv_ref)).astype(BF16)
        part = jnp.dot(act, wd_ref[...], preferred_element_type=F32)
        rs = slice(r * rb, (r + 1) * rb)

        @pl.when(j == 0)
        def _():
            acc_ref[rs, :] = part

        @pl.when(j > 0)
        def _():
            acc_ref[rs, :] += part

    @pl.when(j == n_slabs - 1)
    def _():
        gate = gate_ref[...]
        fnw = fnw_ref[...]

        def body(r, carry):
            rows = pl.ds(pl.multiple_of(r * 32, 32), 32)
            y = x_ref[rows, :] + gate * acc_ref[rows, :]
            o_ref[rows, :] = y * lax.rsqrt(jnp.mean(y * y, axis=-1, keepdims=True) + EPS) * fnw
            return carry

        lax.fori_loop(0, tm // 32, body, 0)


def _ffn_call(x1, norm_w, mod, final_w, w_up_bf, conv_w, conv_b, w_down_bf, tm, tf):
    s = x1.shape[0]
    n_slabs = D_FF // tf
    def vec(k):
        return pl.BlockSpec((1, D_MODEL), lambda i, j: (0, k))

    return pl.pallas_call(
        functools.partial(_ffn_kernel, tm=tm, n_slabs=n_slabs),
        grid=(s // tm, n_slabs),
        in_specs=[pl.BlockSpec((tm, D_MODEL), lambda i, j: (i, 0)),
                  pl.BlockSpec((1, D_MODEL), lambda i, j: (0, 0)),
                  vec(4), vec(3), vec(5),
                  pl.BlockSpec((1, D_MODEL), lambda i, j: (0, 0)),
                  pl.BlockSpec((D_MODEL, tf), lambda i, j: (0, j)),
                  pl.BlockSpec((D_MODEL, tf), lambda i, j: (0, n_slabs + j)),
                  pl.BlockSpec((FFN_CONV, tf), lambda i, j: (0, j)),
                  pl.BlockSpec((FFN_CONV, tf), lambda i, j: (0, n_slabs + j)),
                  pl.BlockSpec((1, tf), lambda i, j: (0, j)),
                  pl.BlockSpec((1, tf), lambda i, j: (0, n_slabs + j)),
                  pl.BlockSpec((tf, D_MODEL), lambda i, j: (j, 0))],
        out_specs=pl.BlockSpec((tm, D_MODEL), lambda i, j: (i, 0)),
        out_shape=jax.ShapeDtypeStruct((s, D_MODEL), F32),
        scratch_shapes=[pltpu.VMEM((tm, D_MODEL), BF16),
                        pltpu.VMEM((tm, D_MODEL), F32),
                        pltpu.VMEM((tm + HALO, tf), F32),
                        pltpu.VMEM((tm + HALO, tf), F32),
                        pltpu.VMEM((n_slabs, 2, HALO, tf), F32)],
        compiler_params=_cparams(2),
        name="ffn",
    )(x1, norm_w, mod, mod, mod, final_w, w_up_bf, w_up_bf, conv_w, conv_w, conv_b, conv_b, w_down_bf)


def _tri3(n):
    tri = jnp.tril(jnp.ones((n, n), F32)).astype(BF16)
    return jnp.concatenate([tri, tri, tri], axis=1)


def _pad_lanes(v):
    return jnp.pad(v.reshape(1, -1), ((0, 0), (0, LANES - v.shape[-1])))


def kernel(x, c, w_mod, b_mod, norm1_w, w_in, hgrn_lb, hgrn_gnorm_w, ssd_conv_w, ssd_conv_b, ssd_dt_bias,
           ssd_a_log, ssd_d, ssd_norm_w, w_out, norm2_w, ffn_w_up, ffn_conv_w, ffn_conv_b, ffn_w_down,
           final_norm_w):
    bsz, seq, _ = x.shape
    assert bsz == 1 and w_in.shape[0] == 1, "single batch element, single layer"
    assert seq % CHUNK == 0
    x2 = x.reshape(seq, D_MODEL)
    tm_in = min(seq, 1024)
    tile_mix = min(seq, 256)
    tm_out = min(seq, 512)
    tm_ffn = min(seq, 512)

    mod = _mod_call(c.reshape(D_MODEL, 1), w_mod[0], b_mod)

    w_dt = jnp.pad(w_in[0][:, D_MAIN:], ((0, 0), (0, LANES - SSM_HEADS))).astype(BF16)
    proj, dt_raw = _inproj_call(x2, norm1_w, mod, w_in[0], w_dt, tm_in, 512)

    tri3 = _tri3(CHUNK)
    o_a = _hgrn_call(proj, hgrn_lb, hgrn_gnorm_w, tri3, tile_mix)
    o_b = _ssd_call(proj, dt_raw, ssd_conv_w[0], ssd_conv_b, _pad_lanes(ssd_dt_bias[0]),
                    _pad_lanes(ssd_a_log[0]), jnp.repeat(ssd_d[0], SSM_P).reshape(1, D_SSM),
                    ssd_norm_w, tri3, tile_mix)

    x1 = _outproj_call(x2, o_a, o_b, w_out[0].astype(BF16), mod, tm_out)

    out = _ffn_call(x1, norm2_w, mod, final_norm_w.reshape(1, D_MODEL), ffn_w_up[0].astype(BF16),
                    ffn_conv_w[0], ffn_conv_b, ffn_w_down[0].astype(BF16), tm_ffn, 512)
    return out.reshape(bsz, seq, D_MODEL)
```

```python
import functools
import math

import jax
import jax.numpy as jnp
from jax import lax
from jax.experimental import pallas as pl
from jax.experimental.pallas import tpu as pltpu

F32 = jnp.float32
BF16 = jnp.bfloat16

D_MODEL = 2048
D_HGRN = 1024
HGRN_HEADS = 8
HEAD_K = 128
D_SSM = 1024
SSM_HEADS = 16
SSM_P = 64
SSM_GROUPS = 4
SSM_N = 128
SSM_CONV = 4
D_MAIN = 4 * D_HGRN + D_SSM + (D_SSM + 2 * SSM_GROUPS * SSM_N)
D_FF = 5632
FFN_CONV = 3
EPS = 1e-6
LOG2E = math.log2(math.e)

LANES = 128
SUBLANES = 8
CHUNK = 128
HALO = SUBLANES
VMEM_LIMIT = 60 * 1024 * 1024

NT_DIMS = (((1,), (1,)), ((), ()))


def _cparams(n_axes):
    return pltpu.CompilerParams(dimension_semantics=("arbitrary",) * n_axes,
                                vmem_limit_bytes=VMEM_LIMIT)


def _sigmoid(x):
    return 1.0 / (1.0 + jnp.exp(-x))


def _silu(x):
    return x * _sigmoid(x)


def _split_bf16(x, n_terms):
    terms = []
    r = x
    for t in range(n_terms):
        p = r.astype(BF16)
        terms.append(p)
        if t + 1 < n_terms:
            r = r - p.astype(F32)
    return terms


def _cumsum_rows(tri3, x):
    return jnp.dot(tri3, jnp.concatenate(_split_bf16(x, 3), axis=0), preferred_element_type=F32)


def _mod_kernel(c_ref, w_ref, b_ref, o_ref):
    s = _silu(c_ref[...])
    o_ref[...] = jnp.sum(w_ref[...] * s, axis=0, keepdims=True) + b_ref[...]


def _mod_call(c_col, w_mod, b_mod):
    d, n = w_mod.shape
    tn = 1024
    return pl.pallas_call(
        _mod_kernel,
        grid=(n // tn,),
        in_specs=[pl.BlockSpec((d, 1), lambda j: (0, 0)),
                  pl.BlockSpec((d, tn), lambda j: (0, j)),
                  pl.BlockSpec((1, tn), lambda j: (0, j))],
        out_specs=pl.BlockSpec((1, tn), lambda j: (0, j)),
        out_shape=jax.ShapeDtypeStruct((1, n), F32),
        compiler_params=_cparams(1),
        name="mod",
    )(c_col, w_mod, b_mod)


def _norm_mod_rows(x_ref, nw_ref, sc_ref, sh_ref, h_ref, n_rows, row_block=32):
    nw = nw_ref[...]
    sc = 1.0 + sc_ref[...]
    sh = sh_ref[...]

    def body(r, carry):
        rows = pl.ds(pl.multiple_of(r * row_block, row_block), row_block)
        x = x_ref[rows, :]
        y = x * lax.rsqrt(jnp.mean(x * x, axis=-1, keepdims=True) + EPS)
        h_ref[rows, :] = ((y * nw) * sc + sh).astype(BF16)
        return carry

    lax.fori_loop(0, n_rows // row_block, body, 0)


def _inproj_kernel(x_ref, nw_ref, sc_ref, sh_ref, wt_ref, wdt_ref, o_ref, dt_ref, h_ref, *, tm):
    @pl.when(pl.program_id(1) == 0)
    def _():
        _norm_mod_rows(x_ref, nw_ref, sc_ref, sh_ref, h_ref, tm)
        dt_ref[...] = lax.dot_general(h_ref[...], wdt_ref[...], NT_DIMS, preferred_element_type=F32)

    o_ref[...] = lax.dot_general(h_ref[...], wt_ref[...], NT_DIMS, preferred_element_type=F32).astype(BF16)


def _inproj_call(x2, norm_w, mod, w_t, w_dt_t, tm, tn):
    s = x2.shape[0]
    return pl.pallas_call(
        functools.partial(_inproj_kernel, tm=tm),
        grid=(s // tm, D_MAIN // tn),
        in_specs=[pl.BlockSpec((tm, D_MODEL), lambda i, j: (i, 0)),
                  pl.BlockSpec((1, D_MODEL), lambda i, j: (0, 0)),
                  pl.BlockSpec((1, D_MODEL), lambda i, j: (0, 1)),
                  pl.BlockSpec((1, D_MODEL), lambda i, j: (0, 0)),
                  pl.BlockSpec((tn, D_MODEL), lambda i, j: (j, 0)),
                  pl.BlockSpec((LANES, D_MODEL), lambda i, j: (0, 0))],
        out_specs=[pl.BlockSpec((tm, tn), lambda i, j: (i, j)),
                   pl.BlockSpec((tm, LANES), lambda i, j: (i, 0))],
        out_shape=[jax.ShapeDtypeStruct((s, D_MAIN), BF16),
                   jax.ShapeDtypeStruct((s, LANES), F32)],
        scratch_shapes=[pltpu.VMEM((tm, D_MODEL), BF16)],
        compiler_params=_cparams(2),
        name="inproj",
    )(x2, norm_w, mod, mod, w_t, w_dt_t)


def _row_bcast(b_ref, h, row, n_rows):
    return jnp.broadcast_to(b_ref[h, row:row + 1, :], (n_rows, LANES))


def _hgrn_kernel(q_ref, f_ref, v_ref, g_ref, lb_ref, gnw_ref, tri_ref, o_ref, st_ref, b_ref, *, n_chunks):
    c_len = CHUNK
    n_levels = c_len.bit_length() - 1

    @pl.when(pl.program_id(0) == 0)
    def _():
        st_ref[...] = jnp.zeros_like(st_ref)

    row2 = lax.broadcasted_iota(jnp.int32, (c_len, c_len), 0)
    col2 = lax.broadcasted_iota(jnp.int32, (c_len, c_len), 1)
    xor2 = row2 ^ col2
    level = jnp.where(row2 == col2, n_levels, -1)
    for lvl in range(n_levels):
        level = jnp.where(((xor2 >> lvl) == 1) & (row2 > col2), lvl, level)
    rowk = lax.broadcasted_iota(jnp.int32, (c_len, LANES), 0)
    sub8 = lax.broadcasted_iota(jnp.int32, (SUBLANES, LANES), 0)
    tri3 = tri_ref[...]

    def chunk(c, carry):
        rows = pl.ds(pl.multiple_of(c * c_len, c_len), c_len)
        for h in range(HGRN_HEADS):
            cs = slice(h * HEAD_K, (h + 1) * HEAD_K)
            a0 = lb_ref[0:1, cs]
            a1 = lb_ref[1:2, cs]
            amax = jnp.maximum(a0, a1)
            e0 = jnp.exp(a0 - amax)
            lb = e0 / (e0 + jnp.exp(a1 - amax))
            qs = _silu(q_ref[rows, cs].astype(F32))
            vb = v_ref[rows, cs]
            fg = lb + (1.0 - lb) * _sigmoid(f_ref[rows, cs].astype(F32))
            k = 1.0 - fg
            b = _cumsum_rows(tri3, jnp.log(fg) * LOG2E)
            b_ref[h] = b

            att = jnp.where(level == n_levels,
                            lax.dot_general(qs.astype(BF16), k.astype(BF16), NT_DIMS,
                                            preferred_element_type=F32), 0.0)
            for lvl in range(n_levels):
                m = 1 << lvl
                if m >= SUBLANES:
                    parts = []
                    for blk in range(c_len // (2 * m)):
                        lo = slice(blk * 2 * m, blk * 2 * m + m)
                        up = slice(blk * 2 * m + m, (blk + 1) * 2 * m)
                        bc = _row_bcast(b_ref, h, blk * 2 * m + m - 1, m)
                        parts.append(k[lo] * jnp.exp2(bc - b[lo]))
                        parts.append(qs[up] * jnp.exp2(b[up] - bc))
                    xl = jnp.concatenate(parts, axis=0)
                else:
                    upper = ((rowk >> lvl) & 1) == 1
                    if lvl == 0:
                        xl = jnp.where(upper, qs * fg, k)
                    else:
                        tiles = []
                        for t8 in range(c_len // SUBLANES):
                            if lvl == 2:
                                tiles.append(_row_bcast(b_ref, h, t8 * 8 + 3, SUBLANES))
                            else:
                                tiles.append(jnp.where(sub8 < 4, _row_bcast(b_ref, h, t8 * 8 + 1, SUBLANES),
                                                       _row_bcast(b_ref, h, t8 * 8 + 5, SUBLANES)))
                        d = b - jnp.concatenate(tiles, axis=0)
                        xl = jnp.where(upper, qs, k) * jnp.exp2(jnp.where(upper, d, -d))
                xb = xl.astype(BF16)
                att = jnp.where(level == lvl,
                                lax.dot_general(xb, xb, NT_DIMS, preferred_element_type=F32), att)

            st = st_ref[h]
            qc = (qs * jnp.exp2(b)).astype(BF16)
            o = jnp.dot(att.astype(BF16), vb, preferred_element_type=F32)
            o = o + lax.dot_general(qc, st.astype(BF16), NT_DIMS, preferred_element_type=F32)
            b_last = b_ref[h, c_len - 1:c_len, :]
            ke = (k * jnp.exp2(b_last - b)).astype(BF16)
            vt = vb.astype(F32).T.astype(BF16)
            st_ref[h] = st * jnp.exp2(b_last) + jnp.dot(vt, ke, preferred_element_type=F32)

            on = o * lax.rsqrt(jnp.mean(o * o, axis=-1, keepdims=True) + EPS)
            o_ref[rows, cs] = (on * gnw_ref[0:1, cs] * _silu(g_ref[rows, cs].astype(F32))).astype(BF16)
        return carry

    lax.fori_loop(0, n_chunks, chunk, 0)


def _hgrn_call(proj, hgrn_lb, gnorm_w, tri3, tile):
    s = proj.shape[0]

    def col(k):
        return pl.BlockSpec((tile, D_HGRN), lambda i: (i, k))

    return pl.pallas_call(
        functools.partial(_hgrn_kernel, n_chunks=tile // CHUNK),
        grid=(s // tile,),
        in_specs=[col(0), col(1), col(2), col(3),
                  pl.BlockSpec((2, D_HGRN), lambda i: (0, 0)),
                  pl.BlockSpec((1, D_HGRN), lambda i: (0, 0)),
                  pl.BlockSpec((CHUNK, 3 * CHUNK), lambda i: (0, 0))],
        out_specs=pl.BlockSpec((tile, D_HGRN), lambda i: (i, 0)),
        out_shape=jax.ShapeDtypeStruct((s, D_HGRN), BF16),
        scratch_shapes=[pltpu.VMEM((HGRN_HEADS, HEAD_K, HEAD_K), F32),
                        pltpu.VMEM((HGRN_HEADS, CHUNK, LANES), F32)],
        compiler_params=_cparams(1),
        name="hgrn",
    )(proj, proj, proj, proj, hgrn_lb, gnorm_w, tri3)


def _ssd_kernel(z_ref, xr_ref, bcr_ref, dt_ref, cwx_ref, cwbc_ref, cbx_ref, cbbc_ref, dtb_ref, alog_ref,
                dexp_ref, nw_ref, tri_ref, ecol_ref, e64_ref, o_ref,
                ht_ref, halo_x, halo_bc, ext_x, ext_bc, cx_ref, cbc_ref,
                acol_s, ecol_s, ats_s, y_s, *, tile):
    c_len = CHUNK
    n_chunks = tile // c_len
    pair_w = 2 * SSM_P

    @pl.when(pl.program_id(0) == 0)
    def _():
        ht_ref[...] = jnp.zeros_like(ht_ref)
        halo_x[...] = jnp.zeros_like(halo_x)
        halo_bc[...] = jnp.zeros_like(halo_bc)

    for raw_ref, halo, ext, cw_ref, cb_ref, out in ((xr_ref, halo_x, ext_x, cwx_ref, cbx_ref, cx_ref),
                                                    (bcr_ref, halo_bc, ext_bc, cwbc_ref, cbbc_ref, cbc_ref)):
        ext[0:HALO, :] = halo[...]
        ext[HALO:HALO + tile, :] = raw_ref[...].astype(F32)
        halo[...] = ext[tile:tile + HALO, :]
        rb = 64
        for r in range(tile // rb):
            acc = cb_ref[...] + cw_ref[SSM_CONV - 1:SSM_CONV, :] * ext[HALO + r * rb:HALO + (r + 1) * rb, :]
            for j in range(SSM_CONV - 1):
                off = HALO + r * rb - (SSM_CONV - 1) + j
                acc = acc + cw_ref[j:j + 1, :] * ext[off:off + rb, :]
            out[r * rb:(r + 1) * rb, :] = _silu(acc)

    lane = lax.broadcasted_iota(jnp.int32, (c_len, LANES), 1)
    row2 = lax.broadcasted_iota(jnp.int32, (c_len, c_len), 0)
    col2 = lax.broadcasted_iota(jnp.int32, (c_len, c_len), 1)
    causal = row2 >= col2
    first_head = lane < SSM_P
    head_lane = lax.broadcasted_iota(jnp.int32, (1, LANES), 1) < SSM_HEADS
    a_head = jnp.where(head_lane, -jnp.exp(alog_ref[...]) * LOG2E, 0.0)
    tri3 = tri_ref[...]

    def chunk(c, carry):
        rows = pl.ds(pl.multiple_of(c * c_len, c_len), c_len)
        dt = jax.nn.softplus(dt_ref[rows, :] + dtb_ref[...])
        acum = _cumsum_rows(tri3, dt * a_head)
        a_last = acum[c_len - 1:c_len, :]
        acol_s[...] = jnp.dot(jnp.concatenate(_split_bf16(acum, 3), axis=1), ecol_ref[...],
                              preferred_element_type=F32)
        ecol_s[...] = jnp.dot(jnp.concatenate(_split_bf16(jnp.exp2(acum), 2), axis=1),
                              ecol_ref[0:2 * LANES, :], preferred_element_type=F32)
        wdec = dt * jnp.exp2(a_last - acum)
        wexp = jnp.dot(jnp.concatenate(_split_bf16(wdec, 2), axis=1), e64_ref[...],
                       preferred_element_type=F32)
        ats_s[...] = (acum - jnp.log(dt) * LOG2E).T

        for g in range(SSM_GROUPS):
            bg = cbc_ref[rows, g * SSM_N:(g + 1) * SSM_N]
            cg = cbc_ref[rows, (SSM_GROUPS + g) * SSM_N:(SSM_GROUPS + g + 1) * SSM_N]
            cb = lax.dot_general(cg.astype(BF16), bg.astype(BF16), NT_DIMS, preferred_element_type=F32)
            bgt = bg.T.astype(BF16)
            for pp in range(2):
                pair = 2 * g + pp
                ha, hb = 2 * pair, 2 * pair + 1
                ps = slice(pair * pair_w, (pair + 1) * pair_w)
                xp = cx_ref[rows, ps]
                xpb = xp.astype(BF16)
                zero = jnp.zeros_like(xpb)
                y = None
                for hh, keep in ((ha, first_head), (hb, ~first_head)):
                    ex = acol_s[:, hh * LANES:(hh + 1) * LANES] - ats_s[hh:hh + 1, :]
                    m = (cb * jnp.exp2(jnp.where(causal, ex, -jnp.inf))).astype(BF16)
                    yh = jnp.dot(m, jnp.where(keep, xpb, zero), preferred_element_type=F32)
                    y = yh if y is None else y + yh
                ce = jnp.concatenate([(cg * ecol_s[:, ha * LANES:(ha + 1) * LANES]).astype(BF16),
                                      (cg * ecol_s[:, hb * LANES:(hb + 1) * LANES]).astype(BF16)], axis=1)
                ht = ht_ref[pair]
                y = y + jnp.dot(ce, ht.astype(BF16), preferred_element_type=F32)

                xw = (xp * wexp[:, ps]).astype(BF16)
                dca = ecol_s[c_len - 1:c_len, ha * LANES:(ha + 1) * LANES]
                dcb = ecol_s[c_len - 1:c_len, hb * LANES:(hb + 1) * LANES]
                ht_ref[pair, 0:SSM_N, :] = ht[0:SSM_N] * dca + jnp.dot(
                    bgt, jnp.where(first_head, xw, zero), preferred_element_type=F32)
                ht_ref[pair, SSM_N:2 * SSM_N, :] = ht[SSM_N:] * dcb + jnp.dot(
                    bgt, jnp.where(first_head, zero, xw), preferred_element_type=F32)

                y = y + dexp_ref[0:1, ps] * xp
                y_s[:, ps] = y * _silu(z_ref[rows, ps].astype(F32))

            gs = slice(g * 2 * pair_w, (g + 1) * 2 * pair_w)
            yg = y_s[:, gs]
            yn = yg * lax.rsqrt(jnp.mean(yg * yg, axis=-1, keepdims=True) + EPS)
            o_ref[rows, gs] = (yn * nw_ref[0:1, gs]).astype(BF16)
        return carry

    lax.fori_loop(0, n_chunks, chunk, 0)


def _ssd_call(proj, dt_raw, conv_w, conv_b, dt_bias, a_log, d_exp, norm_w, tri3, e_col, e_64, tile):
    s = proj.shape[0]

    def col(k):
        return pl.BlockSpec((tile, D_SSM), lambda i: (i, k))

    def full(shape):
        return pl.BlockSpec(shape, lambda i: (0, 0))

    n_pairs = SSM_HEADS // 2
    return pl.pallas_call(
        functools.partial(_ssd_kernel, tile=tile),
        grid=(s // tile,),
        in_specs=[col(4), col(5), col(6),
                  pl.BlockSpec((tile, LANES), lambda i: (i, 0)),
                  pl.BlockSpec((SSM_CONV, D_SSM), lambda i: (0, 0)),
                  pl.BlockSpec((SSM_CONV, D_SSM), lambda i: (0, 1)),
                  pl.BlockSpec((1, D_SSM), lambda i: (0, 0)),
                  pl.BlockSpec((1, D_SSM), lambda i: (0, 1)),
                  full((1, LANES)), full((1, LANES)), full((1, D_SSM)), full((1, D_SSM)),
                  full((CHUNK, 3 * CHUNK)), full(e_col.shape), full(e_64.shape)],
        out_specs=pl.BlockSpec((tile, D_SSM), lambda i: (i, 0)),
        out_shape=jax.ShapeDtypeStruct((s, D_SSM), BF16),
        scratch_shapes=[pltpu.VMEM((n_pairs, 2 * SSM_N, LANES), F32),
                        pltpu.VMEM((HALO, D_SSM), F32), pltpu.VMEM((HALO, D_SSM), F32),
                        pltpu.VMEM((tile + HALO, D_SSM), F32), pltpu.VMEM((tile + HALO, D_SSM), F32),
                        pltpu.VMEM((tile, D_SSM), F32), pltpu.VMEM((tile, D_SSM), F32),
                        pltpu.VMEM((CHUNK, SSM_HEADS * LANES), F32), pltpu.VMEM((CHUNK, SSM_HEADS * LANES), F32),
                        pltpu.VMEM((CHUNK, LANES), F32),
                        pltpu.VMEM((CHUNK, D_SSM), F32)],
        compiler_params=_cparams(1),
        name="ssd",
    )(proj, proj, proj, dt_raw, conv_w, conv_w, conv_b, conv_b, dt_bias, a_log, d_exp, norm_w, tri3, e_col, e_64)


def _outproj_kernel(x_ref, oa_ref, ob_ref, wa_ref, wb_ref, g_ref, o_ref):
    mixed = jnp.dot(oa_ref[...], wa_ref[...], preferred_element_type=F32)
    mixed = mixed + jnp.dot(ob_ref[...], wb_ref[...], preferred_element_type=F32)
    o_ref[...] = x_ref[...] + g_ref[...] * mixed


def _outproj_call(x2, o_a, o_b, w_out_bf, mod, tm):
    s = x2.shape[0]
    return pl.pallas_call(
        _outproj_kernel,
        grid=(s // tm,),
        in_specs=[pl.BlockSpec((tm, D_MODEL), lambda i: (i, 0)),
                  pl.BlockSpec((tm, D_HGRN), lambda i: (i, 0)),
                  pl.BlockSpec((tm, D_SSM), lambda i: (i, 0)),
                  pl.BlockSpec((D_HGRN, D_MODEL), lambda i: (0, 0)),
                  pl.BlockSpec((D_SSM, D_MODEL), lambda i: (1, 0)),
                  pl.BlockSpec((1, D_MODEL), lambda i: (0, 2))],
        out_specs=pl.BlockSpec((tm, D_MODEL), lambda i: (i, 0)),
        out_shape=jax.ShapeDtypeStruct((s, D_MODEL), F32),
        compiler_params=_cparams(1),
        name="outproj",
    )(x2, o_a, o_b, w_out_bf, w_out_bf, mod)


def _ffn_kernel(x_ref, nw_ref, sc_ref, sh_ref, gate_ref, fnw_ref, wug_ref, wuv_ref, cwg_ref, cwv_ref,
                cbg_ref, cbv_ref, wd_ref, o_ref, h_ref, ug_ref, uv_ref, carry_ref, *, tm, tf, n_slabs, n_split):
    i = pl.program_id(0)
    j = pl.program_id(1)

    @pl.when(j == 0)
    def _():
        _norm_mod_rows(x_ref, nw_ref, sc_ref, sh_ref, h_ref, tm)
        o_ref[...] = jnp.zeros_like(o_ref)

    @pl.when(i == 0)
    def _():
        carry_ref[j] = jnp.zeros(carry_ref.shape[1:], F32)

    h = h_ref[...]
    tw = tf // n_split
    rb = 128
    part = None
    for sp in range(n_split):
        cs = slice(sp * tw, (sp + 1) * tw)
        for u_ref, w_ref, which in ((ug_ref, wug_ref, 0), (uv_ref, wuv_ref, 1)):
            u_ref[sp, 0:HALO, :] = carry_ref[j, which, sp]
            u_ref[sp, HALO:HALO + tm, :] = jnp.dot(h, w_ref[:, cs], preferred_element_type=F32)
            carry_ref[j, which, sp] = u_ref[sp, tm:tm + HALO, :]

        def conv(u_ref, cw_ref, cb_ref, r):
            acc = cb_ref[:, cs] + cw_ref[FFN_CONV - 1:FFN_CONV, cs] * u_ref[sp, HALO + r * rb:HALO + (r + 1) * rb, :]
            for t in range(FFN_CONV - 1):
                off = HALO + r * rb - (FFN_CONV - 1) + t
                acc = acc + cw_ref[t:t + 1, cs] * u_ref[sp, off:off + rb, :]
            return acc

        act = jnp.concatenate(
            [(_silu(conv(ug_ref, cwg_ref, cbg_ref, r)) * conv(uv_ref, cwv_ref, cbv_ref, r)).astype(BF16)
             for r in range(tm // rb)], axis=0)
        d = jnp.dot(act, wd_ref[cs, :], preferred_element_type=F32)
        part = d if part is None else part + d
    o_ref[...] += part

    @pl.when(j == n_slabs - 1)
    def _():
        gate = gate_ref[...]
        fnw = fnw_ref[...]

        def body(r, carry):
            rows = pl.ds(pl.multiple_of(r * 32, 32), 32)
            y = x_ref[rows, :] + gate * o_ref[rows, :]
            o_ref[rows, :] = y * lax.rsqrt(jnp.mean(y * y, axis=-1, keepdims=True) + EPS) * fnw
            return carry

        lax.fori_loop(0, tm // 32, body, 0)


def _ffn_call(x1, norm_w, mod, final_w, w_up_bf, conv_w, conv_b, w_down_bf, tm, tf, n_split):
    s = x1.shape[0]
    n_slabs = D_FF // tf

    def vec(k):
        return pl.BlockSpec((1, D_MODEL), lambda i, j: (0, k))

    return pl.pallas_call(
        functools.partial(_ffn_kernel, tm=tm, tf=tf, n_slabs=n_slabs, n_split=n_split),
        grid=(s // tm, n_slabs),
        in_specs=[pl.BlockSpec((tm, D_MODEL), lambda i, j: (i, 0)),
                  pl.BlockSpec((1, D_MODEL), lambda i, j: (0, 0)),
                  vec(4), vec(3), vec(5),
                  pl.BlockSpec((1, D_MODEL), lambda i, j: (0, 0)),
                  pl.BlockSpec((D_MODEL, tf), lambda i, j: (0, j)),
                  pl.BlockSpec((D_MODEL, tf), lambda i, j: (0, n_slabs + j)),
                  pl.BlockSpec((FFN_CONV, tf), lambda i, j: (0, j)),
                  pl.BlockSpec((FFN_CONV, tf), lambda i, j: (0, n_slabs + j)),
                  pl.BlockSpec((1, tf), lambda i, j: (0, j)),
                  pl.BlockSpec((1, tf), lambda i, j: (0, n_slabs + j)),
                  pl.BlockSpec((tf, D_MODEL), lambda i, j: (j, 0))],
        out_specs=pl.BlockSpec((tm, D_MODEL), lambda i, j: (i, 0)),
        out_shape=jax.ShapeDtypeStruct((s, D_MODEL), F32),
        scratch_shapes=[pltpu.VMEM((tm, D_MODEL), BF16),
                        pltpu.VMEM((n_split, tm + HALO, tf // n_split), F32),
                        pltpu.VMEM((n_split, tm + HALO, tf // n_split), F32),
                        pltpu.VMEM((n_slabs, 2, n_split, HALO, tf // n_split), F32)],
        compiler_params=_cparams(2),
        name="ffn",
    )(x1, norm_w, mod, mod, mod, final_w, w_up_bf, w_up_bf, conv_w, conv_w, conv_b, conv_b, w_down_bf)


def _tri3(n):
    tri = jnp.tril(jnp.ones((n, n), F32)).astype(BF16)
    return jnp.concatenate([tri, tri, tri], axis=1)


def _expand_matrix(n_terms, width):
    src = jnp.arange(n_terms * LANES) % LANES
    dst = jnp.arange(SSM_HEADS * width) // width
    return (src[:, None] == dst[None, :]).astype(BF16)


def _pad_lanes(v):
    return jnp.pad(v.reshape(1, -1), ((0, 0), (0, LANES - v.shape[-1])))


def kernel(x, c, w_mod, b_mod, norm1_w, w_in, hgrn_lb, hgrn_gnorm_w, ssd_conv_w, ssd_conv_b, ssd_dt_bias,
           ssd_a_log, ssd_d, ssd_norm_w, w_out, norm2_w, ffn_w_up, ffn_conv_w, ffn_conv_b, ffn_w_down,
           final_norm_w):
    bsz, seq, _ = x.shape
    assert bsz == 1 and w_in.shape[0] == 1, "single batch element, single layer"
    assert seq % CHUNK == 0
    x2 = x.reshape(seq, D_MODEL)
    tm_in = min(seq, 1024)
    tile_mix = min(seq, 256)
    tm_out = min(seq, 512)
    tm_ffn = min(seq, 1024)

    mod = _mod_call(c.reshape(D_MODEL, 1), w_mod[0], b_mod)

    w_t = jnp.swapaxes(w_in, 1, 2)[0].astype(BF16)
    w_dt_t = jnp.pad(w_t[D_MAIN:], ((0, LANES - SSM_HEADS), (0, 0)))
    proj, dt_raw = _inproj_call(x2, norm1_w, mod, w_t, w_dt_t, tm_in, 1024)

    tri3 = _tri3(CHUNK)
    o_a = _hgrn_call(proj, hgrn_lb, hgrn_gnorm_w, tri3, tile_mix)
    o_b = _ssd_call(proj, dt_raw, ssd_conv_w[0], ssd_conv_b, _pad_lanes(ssd_dt_bias[0]),
                    _pad_lanes(ssd_a_log[0]), jnp.repeat(ssd_d[0], SSM_P).reshape(1, D_SSM),
                    ssd_norm_w, tri3, _expand_matrix(3, LANES), _expand_matrix(2, SSM_P), tile_mix)

    x1 = _outproj_call(x2, o_a, o_b, w_out[0].astype(BF16), mod, tm_out)

    out = _ffn_call(x1, norm2_w, mod, final_norm_w.reshape(1, D_MODEL), ffn_w_up[0].astype(BF16),
                    ffn_conv_w[0], ffn_conv_b, ffn_w_down[0].astype(BF16), tm_ffn, 512, 2)
    return out.reshape(bsz, seq, D_MODEL)
```

```python
import functools
import math

import jax
import jax.numpy as jnp
from jax import lax
from jax.experimental import pallas as pl
from jax.experimental.pallas import tpu as pltpu

F32 = jnp.float32
BF16 = jnp.bfloat16

D_MODEL = 2048
D_HGRN = 1024
HGRN_HEADS = 8
HEAD_K = 128
D_SSM = 1024
SSM_HEADS = 16
SSM_P = 64
SSM_GROUPS = 4
SSM_N = 128
SSM_CONV = 4
D_MAIN = 4 * D_HGRN + D_SSM + (D_SSM + 2 * SSM_GROUPS * SSM_N)
D_FF = 5632
FFN_CONV = 3
EPS = 1e-6
LOG2E = math.log2(math.e)

LANES = 128
SUBLANES = 8
BF16_ROWS = 16
CHUNK = 128
HALO = SUBLANES
VMEM_LIMIT = 60 * 1024 * 1024

NT_DIMS = (((1,), (1,)), ((), ()))


def _cparams(n_axes, flags=None):
    return pltpu.CompilerParams(dimension_semantics=("arbitrary",) * n_axes,
                                vmem_limit_bytes=VMEM_LIMIT, flags=flags)


def _sigmoid(x):
    return 1.0 / (1.0 + jnp.exp(-x))


def _silu(x):
    return x * _sigmoid(x)


def _split_bf16(x, n_terms):
    terms = []
    r = x
    for t in range(n_terms):
        p = r.astype(BF16)
        terms.append(p)
        if t + 1 < n_terms:
            r = r - p.astype(F32)
    return terms


def _cumsum_rows(tri3, x):
    return jnp.dot(tri3, jnp.concatenate(_split_bf16(x, 3), axis=0), preferred_element_type=F32)


def _mod_kernel(c_ref, w_ref, b_ref, o_ref):
    s = _silu(c_ref[...])
    o_ref[...] = jnp.sum(w_ref[...] * s, axis=0, keepdims=True) + b_ref[...]


def _mod_call(c_col, w_mod, b_mod):
    d, n = w_mod.shape
    tn = 1024
    return pl.pallas_call(
        _mod_kernel,
        grid=(n // tn,),
        in_specs=[pl.BlockSpec((d, 1), lambda j: (0, 0)),
                  pl.BlockSpec((d, tn), lambda j: (0, j)),
                  pl.BlockSpec((1, tn), lambda j: (0, j))],
        out_specs=pl.BlockSpec((1, tn), lambda j: (0, j)),
        out_shape=jax.ShapeDtypeStruct((1, n), F32),
        compiler_params=_cparams(1),
        name="mod",
    )(c_col, w_mod, b_mod)


def _norm_mod_rows(x_ref, nw_ref, sc_ref, sh_ref, h_ref, n_rows, row_block=128):
    nw = nw_ref[...]
    sc = 1.0 + sc_ref[...]
    sh = sh_ref[...]

    def body(r, carry):
        rows = pl.ds(pl.multiple_of(r * row_block, row_block), row_block)
        x = x_ref[rows, :]
        y = x * lax.rsqrt(jnp.mean(x * x, axis=-1, keepdims=True) + EPS)
        h_ref[rows, :] = ((y * nw) * sc + sh).astype(BF16)
        return carry

    lax.fori_loop(0, n_rows // row_block, body, 0)


def _inproj_kernel(x_ref, nw_ref, sc_ref, sh_ref, w_ref, wdt_ref, o_ref, dt_ref, h_ref, *, tm):
    @pl.when(pl.program_id(1) == 0)
    def _():
        _norm_mod_rows(x_ref, nw_ref, sc_ref, sh_ref, h_ref, tm)
        dt_ref[...] = jnp.dot(h_ref[...], wdt_ref[...], preferred_element_type=F32)

    o_ref[...] = jnp.dot(h_ref[...], w_ref[...], preferred_element_type=F32).astype(BF16)


def _inproj_call(x2, norm_w, mod, w_bf, w_dt, tm, tn):
    s = x2.shape[0]
    return pl.pallas_call(
        functools.partial(_inproj_kernel, tm=tm),
        grid=(s // tm, D_MAIN // tn),
        in_specs=[pl.BlockSpec((tm, D_MODEL), lambda i, j: (i, 0)),
                  pl.BlockSpec((1, D_MODEL), lambda i, j: (0, 0)),
                  pl.BlockSpec((1, D_MODEL), lambda i, j: (0, 1)),
                  pl.BlockSpec((1, D_MODEL), lambda i, j: (0, 0)),
                  pl.BlockSpec((D_MODEL, tn), lambda i, j: (0, j)),
                  pl.BlockSpec((D_MODEL, LANES), lambda i, j: (0, 0))],
        out_specs=[pl.BlockSpec((tm, tn), lambda i, j: (i, j)),
                   pl.BlockSpec((tm, LANES), lambda i, j: (i, 0))],
        out_shape=[jax.ShapeDtypeStruct((s, D_MAIN), BF16),
                   jax.ShapeDtypeStruct((s, LANES), F32)],
        scratch_shapes=[pltpu.VMEM((tm, D_MODEL), BF16)],
        compiler_params=_cparams(2),
        name="inproj",
    )(x2, norm_w, mod, mod, w_bf, w_dt)


def _row_bcast(b_ref, h, row, n_rows):
    return jnp.broadcast_to(b_ref[h, row:row + 1, :], (n_rows, LANES))


def _hgrn_kernel(q_ref, f_ref, v_ref, g_ref, lb_ref, gnw_ref, tri_ref, wup_ref, o_ref, wup_bf_ref,
                 st_ref, b_ref, *, n_chunks):
    c_len = CHUNK
    n_levels = c_len.bit_length() - 1

    @pl.when(pl.program_id(0) == 0)
    def _():
        st_ref[...] = jnp.zeros_like(st_ref)

    wup_bf_ref[...] = wup_ref[...].astype(BF16)

    row2 = lax.broadcasted_iota(jnp.int32, (c_len, c_len), 0)
    col2 = lax.broadcasted_iota(jnp.int32, (c_len, c_len), 1)
    xor2 = row2 ^ col2
    level = jnp.where(row2 == col2, n_levels, -1)
    for lvl in range(n_levels):
        level = jnp.where(((xor2 >> lvl) == 1) & (row2 > col2), lvl, level)
    rowk = lax.broadcasted_iota(jnp.int32, (c_len, LANES), 0)
    sub8 = lax.broadcasted_iota(jnp.int32, (SUBLANES, LANES), 0)
    tri3 = tri_ref[...]

    def chunk(c, carry):
        rows = pl.ds(pl.multiple_of(c * c_len, c_len), c_len)
        for h in range(HGRN_HEADS):
            cs = slice(h * HEAD_K, (h + 1) * HEAD_K)
            a0 = lb_ref[0:1, cs]
            a1 = lb_ref[1:2, cs]
            amax = jnp.maximum(a0, a1)
            e0 = jnp.exp(a0 - amax)
            lb = e0 / (e0 + jnp.exp(a1 - amax))
            qs = _silu(q_ref[rows, cs].astype(F32))
            vb = v_ref[rows, cs]
            fg = lb + (1.0 - lb) * _sigmoid(f_ref[rows, cs].astype(F32))
            k = 1.0 - fg
            b = _cumsum_rows(tri3, jnp.log(fg) * LOG2E)
            b_ref[h] = b

            att = jnp.where(level == n_levels,
                            lax.dot_general(qs.astype(BF16), k.astype(BF16), NT_DIMS,
                                            preferred_element_type=F32), 0.0)
            for lvl in range(n_levels):
                m = 1 << lvl
                if m >= SUBLANES:
                    parts = []
                    for blk in range(c_len // (2 * m)):
                        lo = slice(blk * 2 * m, blk * 2 * m + m)
                        up = slice(blk * 2 * m + m, (blk + 1) * 2 * m)
                        bc = _row_bcast(b_ref, h, blk * 2 * m + m - 1, m)
                        parts.append(k[lo] * jnp.exp2(bc - b[lo]))
                        parts.append(qs[up] * jnp.exp2(b[up] - bc))
                    xl = jnp.concatenate(parts, axis=0)
                else:
                    upper = ((rowk >> lvl) & 1) == 1
                    if lvl == 0:
                        xl = jnp.where(upper, qs * fg, k)
                    else:
                        tiles = []
                        for t8 in range(c_len // SUBLANES):
                            if lvl == 2:
                                tiles.append(_row_bcast(b_ref, h, t8 * 8 + 3, SUBLANES))
                            else:
                                tiles.append(jnp.where(sub8 < 4, _row_bcast(b_ref, h, t8 * 8 + 1, SUBLANES),
                                                       _row_bcast(b_ref, h, t8 * 8 + 5, SUBLANES)))
                        d = b - jnp.concatenate(tiles, axis=0)
                        xl = jnp.where(upper, qs, k) * jnp.exp2(jnp.where(upper, d, -d))
                xb = xl.astype(BF16)
                att = jnp.where(level == lvl,
                                lax.dot_general(xb, xb, NT_DIMS, preferred_element_type=F32), att)

            st = st_ref[h]
            qc = (qs * jnp.exp2(b)).astype(BF16)
            o = jnp.dot(att.astype(BF16), vb, preferred_element_type=F32)
            o = o + lax.dot_general(qc, st.astype(BF16), NT_DIMS, preferred_element_type=F32)
            b_last = b_ref[h, c_len - 1:c_len, :]
            ke = (k * jnp.exp2(b_last - b)).astype(BF16)
            vt = vb.astype(F32).T.astype(BF16)
            st_ref[h] = st * jnp.exp2(b_last) + jnp.dot(vt, ke, preferred_element_type=F32)

            on = o * lax.rsqrt(jnp.mean(o * o, axis=-1, keepdims=True) + EPS)
            o_ref[rows, cs] = (on * gnw_ref[0:1, cs] * _silu(g_ref[rows, cs].astype(F32))).astype(BF16)
        return carry

    lax.fori_loop(0, n_chunks, chunk, 0)


def _hgrn_call(proj, hgrn_lb, gnorm_w, tri3, w_up, tile):
    s = proj.shape[0]
    n_steps = s // tile
    up_rows, up_cols = w_up.shape
    assert up_rows % (n_steps * BF16_ROWS) == 0

    def col(k):
        return pl.BlockSpec((tile, D_HGRN), lambda i: (i, k))

    return pl.pallas_call(
        functools.partial(_hgrn_kernel, n_chunks=tile // CHUNK),
        grid=(n_steps,),
        in_specs=[col(0), col(1), col(2), col(3),
                  pl.BlockSpec((2, D_HGRN), lambda i: (0, 0)),
                  pl.BlockSpec((1, D_HGRN), lambda i: (0, 0)),
                  pl.BlockSpec((CHUNK, 3 * CHUNK), lambda i: (0, 0)),
                  pl.BlockSpec((up_rows // n_steps, up_cols), lambda i: (i, 0))],
        out_specs=[pl.BlockSpec((tile, D_HGRN), lambda i: (i, 0)),
                   pl.BlockSpec((up_rows // n_steps, up_cols), lambda i: (i, 0))],
        out_shape=[jax.ShapeDtypeStruct((s, D_HGRN), BF16),
                   jax.ShapeDtypeStruct(w_up.shape, BF16)],
        scratch_shapes=[pltpu.VMEM((HGRN_HEADS, HEAD_K, HEAD_K), F32),
                        pltpu.VMEM((HGRN_HEADS, CHUNK, LANES), F32)],
        compiler_params=_cparams(1),
        name="hgrn",
    )(proj, proj, proj, proj, hgrn_lb, gnorm_w, tri3, w_up)


def _ssd_kernel(z_ref, xr_ref, bcr_ref, dt_ref, cwx_ref, cwbc_ref, cbx_ref, cbbc_ref, dtb_ref, alog_ref,
                dexp_ref, nw_ref, tri_ref, ecol_ref, e64_ref, shift_ref, wdn_ref, wout_ref,
                o_ref, wdn_bf_ref, wout_bf_ref,
                ht_ref, halo_x, halo_bc, cx_ref, cbc_ref, acol_s, ecol_s, ats_s, y_s, *, tile):
    c_len = CHUNK
    n_chunks = tile // c_len
    pair_w = 2 * SSM_P

    @pl.when(pl.program_id(0) == 0)
    def _():
        ht_ref[...] = jnp.zeros_like(ht_ref)
        halo_x[...] = jnp.zeros_like(halo_x)
        halo_bc[...] = jnp.zeros_like(halo_bc)

    wdn_bf_ref[...] = wdn_ref[...].astype(BF16)
    wout_bf_ref[...] = wout_ref[...].astype(BF16)

    slab = 512
    for raw_ref, halo, cw_ref, cb_ref, out in ((xr_ref, halo_x, cwx_ref, cbx_ref, cx_ref),
                                               (bcr_ref, halo_bc, cwbc_ref, cbbc_ref, cbc_ref)):
        for r in range(n_chunks):
            for s0 in range(0, D_SSM, slab):
                cols = slice(s0, s0 + slab)
                cur = raw_ref[r * c_len:(r + 1) * c_len, cols]
                prev = halo[:, cols] if r == 0 else raw_ref[(r - 1) * c_len:r * c_len, cols]
                sh = jnp.dot(shift_ref[...], jnp.concatenate([prev, cur], axis=0), preferred_element_type=F32)
                acc = cb_ref[:, cols] + cw_ref[SSM_CONV - 1:SSM_CONV, cols] * cur.astype(F32)
                for j in range(SSM_CONV - 1):
                    acc = acc + cw_ref[j:j + 1, cols] * sh[j * c_len:(j + 1) * c_len]
                out[r * c_len:(r + 1) * c_len, cols] = _silu(acc)
        halo[...] = raw_ref[tile - c_len:tile, :]

    lane = lax.broadcasted_iota(jnp.int32, (c_len, LANES), 1)
    row2 = lax.broadcasted_iota(jnp.int32, (c_len, c_len), 0)
    col2 = lax.broadcasted_iota(jnp.int32, (c_len, c_len), 1)
    causal = row2 >= col2
    first_head = lane < SSM_P
    head_lane = lax.broadcasted_iota(jnp.int32, (1, LANES), 1) < SSM_HEADS
    a_head = jnp.where(head_lane, -jnp.exp(alog_ref[...]) * LOG2E, 0.0)
    tri3 = tri_ref[...]

    def chunk(c, carry):
        rows = pl.ds(pl.multiple_of(c * c_len, c_len), c_len)
        dt = jax.nn.softplus(dt_ref[rows, :] + dtb_ref[...])
        acum = _cumsum_rows(tri3, dt * a_head)
        a_last = acum[c_len - 1:c_len, :]
        acol_s[...] = jnp.dot(jnp.concatenate(_split_bf16(acum, 3), axis=1), ecol_ref[...],
                              preferred_element_type=F32)
        ecol_s[...] = jnp.dot(jnp.concatenate(_split_bf16(jnp.exp2(acum), 2), axis=1),
                              ecol_ref[0:2 * LANES, :], preferred_element_type=F32)
        wdec = dt * jnp.exp2(a_last - acum)
        wexp = jnp.dot(jnp.concatenate(_split_bf16(wdec, 2), axis=1), e64_ref[...],
                       preferred_element_type=F32)
        ats_s[...] = (acum - jnp.log(dt) * LOG2E).T

        for g in range(SSM_GROUPS):
            bg = cbc_ref[rows, g * SSM_N:(g + 1) * SSM_N]
            cg = cbc_ref[rows, (SSM_GROUPS + g) * SSM_N:(SSM_GROUPS + g + 1) * SSM_N]
            cb = lax.dot_general(cg.astype(BF16), bg.astype(BF16), NT_DIMS, preferred_element_type=F32)
            bgt = bg.T.astype(BF16)
            for pp in range(2):
                pair = 2 * g + pp
                ha, hb = 2 * pair, 2 * pair + 1
                ps = slice(pair * pair_w, (pair + 1) * pair_w)
                xp = cx_ref[rows, ps]
                xpb = xp.astype(BF16)
                zero = jnp.zeros_like(xpb)
                y = None
                for hh, keep in ((ha, first_head), (hb, ~first_head)):
                    ex = acol_s[:, hh * LANES:(hh + 1) * LANES] - ats_s[hh:hh + 1, :]
                    m = (cb * jnp.exp2(jnp.where(causal, ex, -jnp.inf))).astype(BF16)
                    yh = jnp.dot(m, jnp.where(keep, xpb, zero), preferred_element_type=F32)
                    y = yh if y is None else y + yh
                ce = jnp.concatenate([(cg * ecol_s[:, ha * LANES:(ha + 1) * LANES]).astype(BF16),
                                      (cg * ecol_s[:, hb * LANES:(hb + 1) * LANES]).astype(BF16)], axis=1)
                ht = ht_ref[pair]
                y = y + jnp.dot(ce, ht.astype(BF16), preferred_element_type=F32)

                xw = (xp * wexp[:, ps]).astype(BF16)
                dca = ecol_s[c_len - 1:c_len, ha * LANES:(ha + 1) * LANES]
                dcb = ecol_s[c_len - 1:c_len, hb * LANES:(hb + 1) * LANES]
                ht_ref[pair, 0:SSM_N, :] = ht[0:SSM_N] * dca + jnp.dot(
                    bgt, jnp.where(first_head, xw, zero), preferred_element_type=F32)
                ht_ref[pair, SSM_N:2 * SSM_N, :] = ht[SSM_N:] * dcb + jnp.dot(
                    bgt, jnp.where(first_head, zero, xw), preferred_element_type=F32)

                y = y + dexp_ref[0:1, ps] * xp
                y_s[:, ps] = y * _silu(z_ref[rows, ps].astype(F32))

            gs = slice(g * 2 * pair_w, (g + 1) * 2 * pair_w)
            yg = y_s[:, gs]
            yn = yg * lax.rsqrt(jnp.mean(yg * yg, axis=-1, keepdims=True) + EPS)
            o_ref[rows, gs] = (yn * nw_ref[0:1, gs]).astype(BF16)
        return carry

    lax.fori_loop(0, n_chunks, chunk, 0)


def _ssd_call(proj, dt_raw, conv_w, conv_b, dt_bias, a_log, d_exp, norm_w, tri3, e_col, e_64, shift3,
              w_down, w_out, tile):
    s = proj.shape[0]
    n_steps = s // tile
    dn_col_blocks = 2 if n_steps % 2 == 0 else 1
    dn_rows = w_down.shape[0] // (n_steps // dn_col_blocks)
    dn_cols = w_down.shape[1] // dn_col_blocks
    out_rows = w_out.shape[0] // n_steps
    assert dn_rows % BF16_ROWS == 0 and out_rows % BF16_ROWS == 0

    def col(k):
        return pl.BlockSpec((tile, D_SSM), lambda i: (i, k))

    def full(shape):
        return pl.BlockSpec(shape, lambda i: (0, 0))

    dn_spec = pl.BlockSpec((dn_rows, dn_cols), lambda i: (i // dn_col_blocks, i % dn_col_blocks))
    out_spec = pl.BlockSpec((out_rows, w_out.shape[1]), lambda i: (i, 0))
    n_pairs = SSM_HEADS // 2
    return pl.pallas_call(
        functools.partial(_ssd_kernel, tile=tile),
        grid=(n_steps,),
        in_specs=[col(4), col(5), col(6),
                  pl.BlockSpec((tile, LANES), lambda i: (i, 0)),
                  pl.BlockSpec((SSM_CONV, D_SSM), lambda i: (0, 0)),
                  pl.BlockSpec((SSM_CONV, D_SSM), lambda i: (0, 1)),
                  pl.BlockSpec((1, D_SSM), lambda i: (0, 0)),
                  pl.BlockSpec((1, D_SSM), lambda i: (0, 1)),
                  full((1, LANES)), full((1, LANES)), full((1, D_SSM)), full((1, D_SSM)),
                  full((CHUNK, 3 * CHUNK)), full(e_col.shape), full(e_64.shape), full(shift3.shape),
                  dn_spec, out_spec],
        out_specs=[pl.BlockSpec((tile, D_SSM), lambda i: (i, 0)), dn_spec, out_spec],
        out_shape=[jax.ShapeDtypeStruct((s, D_SSM), BF16),
                   jax.ShapeDtypeStruct(w_down.shape, BF16),
                   jax.ShapeDtypeStruct(w_out.shape, BF16)],
        scratch_shapes=[pltpu.VMEM((n_pairs, 2 * SSM_N, LANES), F32),
                        pltpu.VMEM((CHUNK, D_SSM), BF16), pltpu.VMEM((CHUNK, D_SSM), BF16),
                        pltpu.VMEM((tile, D_SSM), F32), pltpu.VMEM((tile, D_SSM), F32),
                        pltpu.VMEM((CHUNK, SSM_HEADS * LANES), F32), pltpu.VMEM((CHUNK, SSM_HEADS * LANES), F32),
                        pltpu.VMEM((CHUNK, LANES), F32),
                        pltpu.VMEM((CHUNK, D_SSM), F32)],
        compiler_params=_cparams(1),
        name="ssd",
    )(proj, proj, proj, dt_raw, conv_w, conv_w, conv_b, conv_b, dt_bias, a_log, d_exp, norm_w, tri3,
      e_col, e_64, shift3, w_down, w_out)


def _outproj_kernel(x_ref, oa_ref, ob_ref, wa_ref, wb_ref, g_ref, o_ref):
    mixed = jnp.dot(oa_ref[...], wa_ref[...], preferred_element_type=F32)
    mixed = mixed + jnp.dot(ob_ref[...], wb_ref[...], preferred_element_type=F32)
    o_ref[...] = x_ref[...] + g_ref[...] * mixed


def _outproj_call(x2, o_a, o_b, w_out_bf, mod, tm):
    s = x2.shape[0]
    return pl.pallas_call(
        _outproj_kernel,
        grid=(s // tm,),
        in_specs=[pl.BlockSpec((tm, D_MODEL), lambda i: (i, 0)),
                  pl.BlockSpec((tm, D_HGRN), lambda i: (i, 0)),
                  pl.BlockSpec((tm, D_SSM), lambda i: (i, 0)),
                  pl.BlockSpec((D_HGRN, D_MODEL), lambda i: (0, 0)),
                  pl.BlockSpec((D_SSM, D_MODEL), lambda i: (1, 0)),
                  pl.BlockSpec((1, D_MODEL), lambda i: (0, 2))],
        out_specs=pl.BlockSpec((tm, D_MODEL), lambda i: (i, 0)),
        out_shape=jax.ShapeDtypeStruct((s, D_MODEL), F32),
        compiler_params=_cparams(1),
        name="outproj",
    )(x2, o_a, o_b, w_out_bf, w_out_bf, mod)


def _ffn_kernel(x_ref, nw_ref, sc_ref, sh_ref, gate_ref, fnw_ref, wug_ref, wuv_ref, cwg_ref, cwv_ref,
                cbg_ref, cbv_ref, wd_ref, o_ref, h_ref, ug_ref, uv_ref, act0_ref, act1_ref, carry_ref,
                *, tm, n_slabs):
    i = pl.program_id(0)
    j = pl.program_id(1)
    rb = 128
    act_ref = (act0_ref, act1_ref)

    @pl.when(j == 0)
    def _():
        _norm_mod_rows(x_ref, nw_ref, sc_ref, sh_ref, h_ref, tm)
        o_ref[...] = jnp.zeros_like(o_ref)

    @pl.when((i == 0) & (j < n_slabs))
    def _():
        carry_ref[j] = jnp.zeros(carry_ref.shape[1:], F32)

    def up_conv(slot):
        h = h_ref[...]
        for u_ref, w_ref, which in ((ug_ref, wug_ref, 0), (uv_ref, wuv_ref, 1)):
            u_ref[0:HALO, :] = carry_ref[j, which]
            u_ref[HALO:HALO + tm, :] = jnp.dot(h, w_ref[...], preferred_element_type=F32)
            carry_ref[j, which] = u_ref[tm:tm + HALO, :]

        def conv(u_ref, cw_ref, cb_ref, r):
            acc = cb_ref[...] + cw_ref[FFN_CONV - 1:FFN_CONV, :] * u_ref[HALO + r * rb:HALO + (r + 1) * rb, :]
            for t in range(FFN_CONV - 1):
                off = HALO + r * rb - (FFN_CONV - 1) + t
                acc = acc + cw_ref[t:t + 1, :] * u_ref[off:off + rb, :]
            return acc

        for r in range(tm // rb):
            act_ref[slot][r * rb:(r + 1) * rb, :] = (
                _silu(conv(ug_ref, cwg_ref, cbg_ref, r)) * conv(uv_ref, cwv_ref, cbv_ref, r)).astype(BF16)

    def down(slot):
        o_ref[...] += jnp.dot(act_ref[slot][...], wd_ref[...], preferred_element_type=F32)

    @pl.when(j == 0)
    def _():
        up_conv(0)

    for parity in range(2):
        @pl.when((j > 0) & (j < n_slabs) & (j % 2 == parity))
        def _():
            down(1 - parity)
            up_conv(parity)

    @pl.when(j == n_slabs)
    def _():
        down((n_slabs - 1) % 2)
        gate = gate_ref[...]
        fnw = fnw_ref[...]

        def body(r, carry):
            rows = pl.ds(pl.multiple_of(r * rb, rb), rb)
            y = x_ref[rows, :] + gate * o_ref[rows, :]
            o_ref[rows, :] = y * lax.rsqrt(jnp.mean(y * y, axis=-1, keepdims=True) + EPS) * fnw
            return carry

        lax.fori_loop(0, tm // rb, body, 0)


def _ffn_call(x1, norm_w, mod, final_w, w_up_bf, conv_w, conv_b, w_down_bf, tm, tf):
    s = x1.shape[0]
    n_slabs = D_FF // tf

    def vec(k):
        return pl.BlockSpec((1, D_MODEL), lambda i, j: (0, k))

    def up(j):
        return jnp.minimum(j, n_slabs - 1)

    def dn(j):
        return jnp.maximum(j - 1, 0)

    return pl.pallas_call(
        functools.partial(_ffn_kernel, tm=tm, n_slabs=n_slabs),
        grid=(s // tm, n_slabs + 1),
        in_specs=[pl.BlockSpec((tm, D_MODEL), lambda i, j: (i, 0)),
                  pl.BlockSpec((1, D_MODEL), lambda i, j: (0, 0)),
                  vec(4), vec(3), vec(5),
                  pl.BlockSpec((1, D_MODEL), lambda i, j: (0, 0)),
                  pl.BlockSpec((D_MODEL, tf), lambda i, j: (0, up(j))),
                  pl.BlockSpec((D_MODEL, tf), lambda i, j: (0, n_slabs + up(j))),
                  pl.BlockSpec((FFN_CONV, tf), lambda i, j: (0, up(j))),
                  pl.BlockSpec((FFN_CONV, tf), lambda i, j: (0, n_slabs + up(j))),
                  pl.BlockSpec((1, tf), lambda i, j: (0, up(j))),
                  pl.BlockSpec((1, tf), lambda i, j: (0, n_slabs + up(j))),
                  pl.BlockSpec((tf, D_MODEL), lambda i, j: (dn(j), 0))],
        out_specs=pl.BlockSpec((tm, D_MODEL), lambda i, j: (i, 0)),
        out_shape=jax.ShapeDtypeStruct((s, D_MODEL), F32),
        scratch_shapes=[pltpu.VMEM((tm, D_MODEL), BF16),
                        pltpu.VMEM((tm + HALO, tf), F32), pltpu.VMEM((tm + HALO, tf), F32),
                        pltpu.VMEM((tm, tf), BF16), pltpu.VMEM((tm, tf), BF16),
                        pltpu.VMEM((n_slabs, 2, HALO, tf), F32)],
        compiler_params=_cparams(2),
        name="ffn",
    )(x1, norm_w, mod, mod, mod, final_w, w_up_bf, w_up_bf, conv_w, conv_w, conv_b, conv_b, w_down_bf)


def _tri3(n):
    tri = jnp.tril(jnp.ones((n, n), F32)).astype(BF16)
    return jnp.concatenate([tri, tri, tri], axis=1)


def _expand_matrix(n_terms, width):
    src = jnp.arange(n_terms * LANES) % LANES
    dst = jnp.arange(SSM_HEADS * width) // width
    return (src[:, None] == dst[None, :]).astype(BF16)


def _shift_matrix(n):
    t = jnp.arange((SSM_CONV - 1) * n)
    src = n + t % n - (SSM_CONV - 1) + t // n
    return (src[:, None] == jnp.arange(2 * n)[None, :]).astype(BF16)


def _pad_lanes(v):
    return jnp.pad(v.reshape(1, -1), ((0, 0), (0, LANES - v.shape[-1])))


def kernel(x, c, w_mod, b_mod, norm1_w, w_in, hgrn_lb, hgrn_gnorm_w, ssd_conv_w, ssd_conv_b, ssd_dt_bias,
           ssd_a_log, ssd_d, ssd_norm_w, w_out, norm2_w, ffn_w_up, ffn_conv_w, ffn_conv_b, ffn_w_down,
           final_norm_w):
    bsz, seq, _ = x.shape
    assert bsz == 1 and w_in.shape[0] == 1, "single batch element, single layer"
    assert seq % CHUNK == 0
    x2 = x.reshape(seq, D_MODEL)
    tm_in = min(seq, 1024)
    tile_mix = min(seq, 256)
    tm_out = min(seq, 512)
    tm_ffn = min(seq, 1024)

    mod = _mod_call(c.reshape(D_MODEL, 1), w_mod[0], b_mod)

    w_in_bf = w_in[0].astype(BF16)
    w_dt = jnp.pad(w_in_bf[:, D_MAIN:], ((0, 0), (0, LANES - SSM_HEADS)))
    proj, dt_raw = _inproj_call(x2, norm1_w, mod, w_in_bf, w_dt, tm_in, D_MAIN // 4)

    tri3 = _tri3(CHUNK)
    o_a, w_up_bf = _hgrn_call(proj, hgrn_lb, hgrn_gnorm_w, tri3, ffn_w_up[0], tile_mix)
    o_b, w_down_bf, w_out_bf = _ssd_call(
        proj, dt_raw, ssd_conv_w[0], ssd_conv_b, _pad_lanes(ssd_dt_bias[0]), _pad_lanes(ssd_a_log[0]),
        jnp.repeat(ssd_d[0], SSM_P).reshape(1, D_SSM), ssd_norm_w, tri3, _expand_matrix(3, LANES),
        _expand_matrix(2, SSM_P), _shift_matrix(CHUNK), ffn_w_down[0], w_out[0], tile_mix)

    x1 = _outproj_call(x2, o_a, o_b, w_out_bf, mod, tm_out)

    out = _ffn_call(x1, norm2_w, mod, final_norm_w.reshape(1, D_MODEL), w_up_bf, ffn_conv_w[0], ffn_conv_b,
                    w_down_bf, tm_ffn, 512)
    return out.reshape(bsz, seq, D_MODEL)
```

```python
import functools
import math

import jax
import jax.numpy as jnp
from jax import lax
from jax.experimental import pallas as pl
from jax.experimental.pallas import tpu as pltpu

F32 = jnp.float32
BF16 = jnp.bfloat16

D_MODEL = 2048
D_HGRN = 1024
HGRN_HEADS = 8
HEAD_K = 128
D_SSM = 1024
SSM_HEADS = 16
SSM_P = 64
SSM_GROUPS = 4
SSM_N = 128
SSM_CONV = 4
D_MAIN = 4 * D_HGRN + D_SSM + (D_SSM + 2 * SSM_GROUPS * SSM_N)
D_FF = 5632
FFN_CONV = 3
EPS = 1e-6
LOG2E = math.log2(math.e)

LANES = 128
SUBLANES = 8
BF16_ROWS = 16
CHUNK = 128
FAST_LEVELS = 5
FAST_MAX_LOG2 = 100.0
HALO = SUBLANES
VMEM_LIMIT = 60 * 1024 * 1024

NT_DIMS = (((1,), (1,)), ((), ()))


def _cparams(n_axes, flags=None):
    return pltpu.CompilerParams(dimension_semantics=("arbitrary",) * n_axes,
                                vmem_limit_bytes=VMEM_LIMIT, flags=flags)


def _sigmoid(x):
    return 1.0 / (1.0 + jnp.exp(-x))


def _silu(x):
    return x * _sigmoid(x)


def _split_bf16(x, n_terms):
    terms = []
    r = x
    for t in range(n_terms):
        p = r.astype(BF16)
        terms.append(p)
        if t + 1 < n_terms:
            r = r - p.astype(F32)
    return terms


def _cumsum_rows(tri3, x):
    return jnp.dot(tri3, jnp.concatenate(_split_bf16(x, 3), axis=0), preferred_element_type=F32)


def _mod_kernel(c_ref, w_ref, b_ref, o_ref):
    s = _silu(c_ref[...])
    o_ref[...] = jnp.sum(w_ref[...] * s, axis=0, keepdims=True) + b_ref[...]


def _mod_call(c_col, w_mod, b_mod):
    d, n = w_mod.shape
    tn = 1024
    return pl.pallas_call(
        _mod_kernel,
        grid=(n // tn,),
        in_specs=[pl.BlockSpec((d, 1), lambda j: (0, 0)),
                  pl.BlockSpec((d, tn), lambda j: (0, j)),
                  pl.BlockSpec((1, tn), lambda j: (0, j))],
        out_specs=pl.BlockSpec((1, tn), lambda j: (0, j)),
        out_shape=jax.ShapeDtypeStruct((1, n), F32),
        compiler_params=_cparams(1),
        name="mod",
    )(c_col, w_mod, b_mod)


def _norm_mod_rows(x_ref, nw_ref, sc_ref, sh_ref, h_ref, n_rows, row_block=128):
    nw = nw_ref[...]
    sc = 1.0 + sc_ref[...]
    sh = sh_ref[...]

    def body(r, carry):
        rows = pl.ds(pl.multiple_of(r * row_block, row_block), row_block)
        x = x_ref[rows, :]
        y = x * lax.rsqrt(jnp.mean(x * x, axis=-1, keepdims=True) + EPS)
        h_ref[rows, :] = ((y * nw) * sc + sh).astype(BF16)
        return carry

    lax.fori_loop(0, n_rows // row_block, body, 0)


def _inproj_kernel(x_ref, nw_ref, sc_ref, sh_ref, w_ref, wdt_ref, o_ref, dt_ref, h_ref, *, tm):
    @pl.when(pl.program_id(1) == 0)
    def _():
        _norm_mod_rows(x_ref, nw_ref, sc_ref, sh_ref, h_ref, tm)
        dt_ref[...] = jnp.dot(h_ref[...], wdt_ref[...], preferred_element_type=F32)

    o_ref[...] = jnp.dot(h_ref[...], w_ref[...], preferred_element_type=F32).astype(BF16)


def _inproj_call(x2, norm_w, mod, w_bf, w_dt, tm, tn):
    s = x2.shape[0]
    return pl.pallas_call(
        functools.partial(_inproj_kernel, tm=tm),
        grid=(s // tm, D_MAIN // tn),
        in_specs=[pl.BlockSpec((tm, D_MODEL), lambda i, j: (i, 0)),
                  pl.BlockSpec((1, D_MODEL), lambda i, j: (0, 0)),
                  pl.BlockSpec((1, D_MODEL), lambda i, j: (0, 1)),
                  pl.BlockSpec((1, D_MODEL), lambda i, j: (0, 0)),
                  pl.BlockSpec((D_MODEL, tn), lambda i, j: (0, j)),
                  pl.BlockSpec((D_MODEL, LANES), lambda i, j: (0, 0))],
        out_specs=[pl.BlockSpec((tm, tn), lambda i, j: (i, j)),
                   pl.BlockSpec((tm, LANES), lambda i, j: (i, 0))],
        out_shape=[jax.ShapeDtypeStruct((s, D_MAIN), BF16),
                   jax.ShapeDtypeStruct((s, LANES), F32)],
        scratch_shapes=[pltpu.VMEM((tm, D_MODEL), BF16)],
        compiler_params=_cparams(2),
        name="inproj",
    )(x2, norm_w, mod, mod, w_bf, w_dt)


def _row_bcast(b_ref, h, row, n_rows):
    return jnp.broadcast_to(b_ref[h, row:row + 1, :], (n_rows, LANES))


def _hgrn_kernel(q_ref, f_ref, v_ref, g_ref, lb_ref, gnw_ref, tri_ref, wup_ref, o_ref, wup_bf_ref,
                 st_ref, b_ref, fg_ref, *, n_chunks):
    c_len = CHUNK
    n_levels = c_len.bit_length() - 1

    @pl.when(pl.program_id(0) == 0)
    def _():
        st_ref[...] = jnp.zeros_like(st_ref)

    wup_bf_ref[...] = wup_ref[...].astype(BF16)

    row2 = lax.broadcasted_iota(jnp.int32, (c_len, c_len), 0)
    col2 = lax.broadcasted_iota(jnp.int32, (c_len, c_len), 1)
    xor2 = row2 ^ col2
    level = jnp.where(row2 == col2, n_levels, -1)
    for lvl in range(n_levels):
        level = jnp.where(((xor2 >> lvl) == 1) & (row2 > col2), lvl, level)
    in_block = (level == n_levels) | ((level >= 0) & (level < FAST_LEVELS))
    rowk = lax.broadcasted_iota(jnp.int32, (c_len, LANES), 0)
    sub8 = lax.broadcasted_iota(jnp.int32, (SUBLANES, LANES), 0)
    tri3 = tri_ref[...]

    def block_anchor(h, blk):
        size = 1 << FAST_LEVELS
        return _row_bcast(b_ref, h, blk * size + size // 2 - 1, size)

    def gates(c):
        rows = pl.ds(pl.multiple_of(c * c_len, c_len), c_len)
        worst = jnp.zeros((1 << FAST_LEVELS, LANES), F32)
        for h in range(HGRN_HEADS):
            cs = slice(h * HEAD_K, (h + 1) * HEAD_K)
            a0 = lb_ref[0:1, cs]
            a1 = lb_ref[1:2, cs]
            amax = jnp.maximum(a0, a1)
            e0 = jnp.exp(a0 - amax)
            lb = e0 / (e0 + jnp.exp(a1 - amax))
            fg = lb + (1.0 - lb) * _sigmoid(f_ref[rows, cs].astype(F32))
            fg_ref[h] = fg
            b = _cumsum_rows(tri3, jnp.log(fg) * LOG2E)
            b_ref[h] = b
            for blk in range(c_len >> FAST_LEVELS):
                span = slice(blk << FAST_LEVELS, (blk + 1) << FAST_LEVELS)
                worst = jnp.maximum(worst, jnp.abs(b[span] - block_anchor(h, blk)))
        return jnp.max(worst) <= FAST_MAX_LOG2

    def heads(c, fast):
        rows = pl.ds(pl.multiple_of(c * c_len, c_len), c_len)
        for h in range(HGRN_HEADS):
            cs = slice(h * HEAD_K, (h + 1) * HEAD_K)
            qs = _silu(q_ref[rows, cs].astype(F32))
            vb = v_ref[rows, cs]
            fg = fg_ref[h]
            k = 1.0 - fg
            b = b_ref[h]

            if fast:
                d = b - jnp.concatenate([block_anchor(h, blk) for blk in range(c_len >> FAST_LEVELS)], axis=0)
                att = jnp.where(in_block,
                                lax.dot_general((qs * jnp.exp2(d)).astype(BF16), (k * jnp.exp2(-d)).astype(BF16),
                                                NT_DIMS, preferred_element_type=F32), 0.0)
            else:
                att = jnp.where(level == n_levels,
                                lax.dot_general(qs.astype(BF16), k.astype(BF16), NT_DIMS,
                                                preferred_element_type=F32), 0.0)
            for lvl in range(FAST_LEVELS if fast else 0, n_levels):
                m = 1 << lvl
                if m >= SUBLANES:
                    parts = []
                    for blk in range(c_len // (2 * m)):
                        lo = slice(blk * 2 * m, blk * 2 * m + m)
                        up = slice(blk * 2 * m + m, (blk + 1) * 2 * m)
                        bc = _row_bcast(b_ref, h, blk * 2 * m + m - 1, m)
                        parts.append(k[lo] * jnp.exp2(bc - b[lo]))
                        parts.append(qs[up] * jnp.exp2(b[up] - bc))
                    xl = jnp.concatenate(parts, axis=0)
                else:
                    upper = ((rowk >> lvl) & 1) == 1
                    if lvl == 0:
                        xl = jnp.where(upper, qs * fg, k)
                    else:
                        tiles = []
                        for t8 in range(c_len // SUBLANES):
                            if lvl == 2:
                                tiles.append(_row_bcast(b_ref, h, t8 * 8 + 3, SUBLANES))
                            else:
                                tiles.append(jnp.where(sub8 < 4, _row_bcast(b_ref, h, t8 * 8 + 1, SUBLANES),
                                                       _row_bcast(b_ref, h, t8 * 8 + 5, SUBLANES)))
                        d = b - jnp.concatenate(tiles, axis=0)
                        xl = jnp.where(upper, qs, k) * jnp.exp2(jnp.where(upper, d, -d))
                xb = xl.astype(BF16)
                att = jnp.where(level == lvl,
                                lax.dot_general(xb, xb, NT_DIMS, preferred_element_type=F32), att)

            st = st_ref[h]
            qc = (qs * jnp.exp2(b)).astype(BF16)
            o = jnp.dot(att.astype(BF16), vb, preferred_element_type=F32)
            o = o + lax.dot_general(qc, st.astype(BF16), NT_DIMS, preferred_element_type=F32)
            b_last = b_ref[h, c_len - 1:c_len, :]
            ke = (k * jnp.exp2(b_last - b)).astype(BF16)
            vt = vb.astype(F32).T.astype(BF16)
            st_ref[h] = st * jnp.exp2(b_last) + jnp.dot(vt, ke, preferred_element_type=F32)

            on = o * lax.rsqrt(jnp.mean(o * o, axis=-1, keepdims=True) + EPS)
            o_ref[rows, cs] = (on * gnw_ref[0:1, cs] * _silu(g_ref[rows, cs].astype(F32))).astype(BF16)

    def chunk(c, carry):
        fast_ok = gates(c)

        @pl.when(fast_ok)
        def _():
            heads(c, True)

        @pl.when(jnp.logical_not(fast_ok))
        def _():
            heads(c, False)

        return carry

    lax.fori_loop(0, n_chunks, chunk, 0)


def _hgrn_call(proj, hgrn_lb, gnorm_w, tri3, w_up, tile):
    s = proj.shape[0]
    n_steps = s // tile
    up_rows, up_cols = w_up.shape
    assert up_rows % (n_steps * BF16_ROWS) == 0

    def col(k):
        return pl.BlockSpec((tile, D_HGRN), lambda i: (i, k))

    return pl.pallas_call(
        functools.partial(_hgrn_kernel, n_chunks=tile // CHUNK),
        grid=(n_steps,),
        in_specs=[col(0), col(1), col(2), col(3),
                  pl.BlockSpec((2, D_HGRN), lambda i: (0, 0)),
                  pl.BlockSpec((1, D_HGRN), lambda i: (0, 0)),
                  pl.BlockSpec((CHUNK, 3 * CHUNK), lambda i: (0, 0)),
                  pl.BlockSpec((up_rows // n_steps, up_cols), lambda i: (i, 0))],
        out_specs=[pl.BlockSpec((tile, D_HGRN), lambda i: (i, 0)),
                   pl.BlockSpec((up_rows // n_steps, up_cols), lambda i: (i, 0))],
        out_shape=[jax.ShapeDtypeStruct((s, D_HGRN), BF16),
                   jax.ShapeDtypeStruct(w_up.shape, BF16)],
        scratch_shapes=[pltpu.VMEM((HGRN_HEADS, HEAD_K, HEAD_K), F32),
                        pltpu.VMEM((HGRN_HEADS, CHUNK, LANES), F32),
                        pltpu.VMEM((HGRN_HEADS, CHUNK, LANES), F32)],
        compiler_params=_cparams(1),
        name="hgrn",
    )(proj, proj, proj, proj, hgrn_lb, gnorm_w, tri3, w_up)


def _ssd_kernel(z_ref, xr_ref, bcr_ref, dt_ref, cwx_ref, cwbc_ref, cbx_ref, cbbc_ref, dtb_ref, alog_ref,
                dexp_ref, nw_ref, tri_ref, ecol_ref, e64_ref, shift_ref, wdn_ref, wout_ref,
                o_ref, wdn_bf_ref, wout_bf_ref,
                ht_ref, halo_x, halo_bc, cx_ref, cbc_ref, acol_s, ecol_s, ats_s, y_s, *, tile):
    c_len = CHUNK
    n_chunks = tile // c_len
    pair_w = 2 * SSM_P

    @pl.when(pl.program_id(0) == 0)
    def _():
        ht_ref[...] = jnp.zeros_like(ht_ref)
        halo_x[...] = jnp.zeros_like(halo_x)
        halo_bc[...] = jnp.zeros_like(halo_bc)

    wdn_bf_ref[...] = wdn_ref[...].astype(BF16)
    wout_bf_ref[...] = wout_ref[...].astype(BF16)

    slab = 512
    for raw_ref, halo, cw_ref, cb_ref, out in ((xr_ref, halo_x, cwx_ref, cbx_ref, cx_ref),
                                               (bcr_ref, halo_bc, cwbc_ref, cbbc_ref, cbc_ref)):
        for r in range(n_chunks):
            for s0 in range(0, D_SSM, slab):
                cols = slice(s0, s0 + slab)
                cur = raw_ref[r * c_len:(r + 1) * c_len, cols]
                prev = halo[:, cols] if r == 0 else raw_ref[(r - 1) * c_len:r * c_len, cols]
                sh = jnp.dot(shift_ref[...], jnp.concatenate([prev, cur], axis=0), preferred_element_type=F32)
                acc = cb_ref[:, cols] + cw_ref[SSM_CONV - 1:SSM_CONV, cols] * cur.astype(F32)
                for j in range(SSM_CONV - 1):
                    acc = acc + cw_ref[j:j + 1, cols] * sh[j * c_len:(j + 1) * c_len]
                out[r * c_len:(r + 1) * c_len, cols] = _silu(acc)
        halo[...] = raw_ref[tile - c_len:tile, :]

    lane = lax.broadcasted_iota(jnp.int32, (c_len, LANES), 1)
    row2 = lax.broadcasted_iota(jnp.int32, (c_len, c_len), 0)
    col2 = lax.broadcasted_iota(jnp.int32, (c_len, c_len), 1)
    causal = row2 >= col2
    first_head = lane < SSM_P
    head_lane = lax.broadcasted_iota(jnp.int32, (1, LANES), 1) < SSM_HEADS
    a_head = jnp.where(head_lane, -jnp.exp(alog_ref[...]) * LOG2E, 0.0)
    tri3 = tri_ref[...]

    def chunk(c, carry):
        rows = pl.ds(pl.multiple_of(c * c_len, c_len), c_len)
        dt = jax.nn.softplus(dt_ref[rows, :] + dtb_ref[...])
        acum = _cumsum_rows(tri3, dt * a_head)
        a_last = acum[c_len - 1:c_len, :]
        acol_s[...] = jnp.dot(jnp.concatenate(_split_bf16(acum, 3), axis=1), ecol_ref[...],
                              preferred_element_type=F32)
        ecol_s[...] = jnp.dot(jnp.concatenate(_split_bf16(jnp.exp2(acum), 2), axis=1),
                              ecol_ref[0:2 * LANES, :], preferred_element_type=F32)
        wdec = dt * jnp.exp2(a_last - acum)
        wexp = jnp.dot(jnp.concatenate(_split_bf16(wdec, 2), axis=1), e64_ref[...],
                       preferred_element_type=F32)
        ats_s[...] = (acum - jnp.log(dt) * LOG2E).T

        for g in range(SSM_GROUPS):
            bg = cbc_ref[rows, g * SSM_N:(g + 1) * SSM_N]
            cg = cbc_ref[rows, (SSM_GROUPS + g) * SSM_N:(SSM_GROUPS + g + 1) * SSM_N]
            cb = lax.dot_general(cg.astype(BF16), bg.astype(BF16), NT_DIMS, preferred_element_type=F32)
            bgt = bg.T.astype(BF16)
            for pp in range(2):
                pair = 2 * g + pp
                ha, hb = 2 * pair, 2 * pair + 1
                ps = slice(pair * pair_w, (pair + 1) * pair_w)
                xp = cx_ref[rows, ps]
                xpb = xp.astype(BF16)
                zero = jnp.zeros_like(xpb)
                y = None
                for hh, keep in ((ha, first_head), (hb, ~first_head)):
                    ex = acol_s[:, hh * LANES:(hh + 1) * LANES] - ats_s[hh:hh + 1, :]
                    m = (cb * jnp.exp2(jnp.where(causal, ex, -jnp.inf))).astype(BF16)
                    yh = jnp.dot(m, jnp.where(keep, xpb, zero), preferred_element_type=F32)
                    y = yh if y is None else y + yh
                ce = jnp.concatenate([(cg * ecol_s[:, ha * LANES:(ha + 1) * LANES]).astype(BF16),
                                      (cg * ecol_s[:, hb * LANES:(hb + 1) * LANES]).astype(BF16)], axis=1)
                ht = ht_ref[pair]
                y = y + jnp.dot(ce, ht.astype(BF16), preferred_element_type=F32)

                xw = (xp * wexp[:, ps]).astype(BF16)
                dca = ecol_s[c_len - 1:c_len, ha * LANES:(ha + 1) * LANES]
                dcb = ecol_s[c_len - 1:c_len, hb * LANES:(hb + 1) * LANES]
                ht_ref[pair, 0:SSM_N, :] = ht[0:SSM_N] * dca + jnp.dot(
                    bgt, jnp.where(first_head, xw, zero), preferred_element_type=F32)
                ht_ref[pair, SSM_N:2 * SSM_N, :] = ht[SSM_N:] * dcb + jnp.dot(
                    bgt, jnp.where(first_head, zero, xw), preferred_element_type=F32)

                y = y + dexp_ref[0:1, ps] * xp
                y_s[:, ps] = y * _silu(z_ref[rows, ps].astype(F32))

            gs = slice(g * 2 * pair_w, (g + 1) * 2 * pair_w)
            yg = y_s[:, gs]
            yn = yg * lax.rsqrt(jnp.mean(yg * yg, axis=-1, keepdims=True) + EPS)
            o_ref[rows, gs] = (yn * nw_ref[0:1, gs]).astype(BF16)
        return carry

    lax.fori_loop(0, n_chunks, chunk, 0)


def _ssd_call(proj, dt_raw, conv_w, conv_b, dt_bias, a_log, d_exp, norm_w, tri3, e_col, e_64, shift3,
              w_down, w_out, tile):
    s = proj.shape[0]
    n_steps = s // tile
    dn_col_blocks = 2 if n_steps % 2 == 0 else 1
    dn_rows = w_down.shape[0] // (n_steps // dn_col_blocks)
    dn_cols = w_down.shape[1] // dn_col_blocks
    out_rows = w_out.shape[0] // n_steps
    assert dn_rows % BF16_ROWS == 0 and out_rows % BF16_ROWS == 0

    def col(k):
        return pl.BlockSpec((tile, D_SSM), lambda i: (i, k))

    def full(shape):
        return pl.BlockSpec(shape, lambda i: (0, 0))

    dn_spec = pl.BlockSpec((dn_rows, dn_cols), lambda i: (i // dn_col_blocks, i % dn_col_blocks))
    out_spec = pl.BlockSpec((out_rows, w_out.shape[1]), lambda i: (i, 0))
    n_pairs = SSM_HEADS // 2
    return pl.pallas_call(
        functools.partial(_ssd_kernel, tile=tile),
        grid=(n_steps,),
        in_specs=[col(4), col(5), col(6),
                  pl.BlockSpec((tile, LANES), lambda i: (i, 0)),
                  pl.BlockSpec((SSM_CONV, D_SSM), lambda i: (0, 0)),
                  pl.BlockSpec((SSM_CONV, D_SSM), lambda i: (0, 1)),
                  pl.BlockSpec((1, D_SSM), lambda i: (0, 0)),
                  pl.BlockSpec((1, D_SSM), lambda i: (0, 1)),
                  full((1, LANES)), full((1, LANES)), full((1, D_SSM)), full((1, D_SSM)),
                  full((CHUNK, 3 * CHUNK)), full(e_col.shape), full(e_64.shape), full(shift3.shape),
                  dn_spec, out_spec],
        out_specs=[pl.BlockSpec((tile, D_SSM), lambda i: (i, 0)), dn_spec, out_spec],
        out_shape=[jax.ShapeDtypeStruct((s, D_SSM), BF16),
                   jax.ShapeDtypeStruct(w_down.shape, BF16),
                   jax.ShapeDtypeStruct(w_out.shape, BF16)],
        scratch_shapes=[pltpu.VMEM((n_pairs, 2 * SSM_N, LANES), F32),
                        pltpu.VMEM((CHUNK, D_SSM), BF16), pltpu.VMEM((CHUNK, D_SSM), BF16),
                        pltpu.VMEM((tile, D_SSM), F32), pltpu.VMEM((tile, D_SSM), F32),
                        pltpu.VMEM((CHUNK, SSM_HEADS * LANES), F32), pltpu.VMEM((CHUNK, SSM_HEADS * LANES), F32),
                        pltpu.VMEM((CHUNK, LANES), F32),
                        pltpu.VMEM((CHUNK, D_SSM), F32)],
        compiler_params=_cparams(1),
        name="ssd",
    )(proj, proj, proj, dt_raw, conv_w, conv_w, conv_b, conv_b, dt_bias, a_log, d_exp, norm_w, tri3,
      e_col, e_64, shift3, w_down, w_out)


def _outproj_kernel(x_ref, oa_ref, ob_ref, wa_ref, wb_ref, g_ref, o_ref):
    mixed = jnp.dot(oa_ref[...], wa_ref[...], preferred_element_type=F32)
    mixed = mixed + jnp.dot(ob_ref[...], wb_ref[...], preferred_element_type=F32)
    o_ref[...] = x_ref[...] + g_ref[...] * mixed


def _outproj_call(x2, o_a, o_b, w_out_bf, mod, tm):
    s = x2.shape[0]
    return pl.pallas_call(
        _outproj_kernel,
        grid=(s // tm,),
        in_specs=[pl.BlockSpec((tm, D_MODEL), lambda i: (i, 0)),
                  pl.BlockSpec((tm, D_HGRN), lambda i: (i, 0)),
                  pl.BlockSpec((tm, D_SSM), lambda i: (i, 0)),
                  pl.BlockSpec((D_HGRN, D_MODEL), lambda i: (0, 0)),
                  pl.BlockSpec((D_SSM, D_MODEL), lambda i: (1, 0)),
                  pl.BlockSpec((1, D_MODEL), lambda i: (0, 2))],
        out_specs=pl.BlockSpec((tm, D_MODEL), lambda i: (i, 0)),
        out_shape=jax.ShapeDtypeStruct((s, D_MODEL), F32),
        compiler_params=_cparams(1),
        name="outproj",
    )(x2, o_a, o_b, w_out_bf, w_out_bf, mod)


def _ffn_kernel(x_ref, nw_ref, sc_ref, sh_ref, gate_ref, fnw_ref, wug_ref, wuv_ref, cwg_ref, cwv_ref,
                cbg_ref, cbv_ref, wd_ref, o_ref, h_ref, ug_ref, uv_ref, act0_ref, act1_ref, carry_ref,
                *, tm, n_slabs):
    i = pl.program_id(0)
    j = pl.program_id(1)
    rb = 128
    act_ref = (act0_ref, act1_ref)

    @pl.when(j == 0)
    def _():
        _norm_mod_rows(x_ref, nw_ref, sc_ref, sh_ref, h_ref, tm)
        o_ref[...] = jnp.zeros_like(o_ref)

    @pl.when((i == 0) & (j < n_slabs))
    def _():
        carry_ref[j] = jnp.zeros(carry_ref.shape[1:], F32)

    def up():
        h = h_ref[...]
        for u_ref, w_ref, which in ((ug_ref, wug_ref, 0), (uv_ref, wuv_ref, 1)):
            u_ref[0:HALO, :] = carry_ref[j, which]
            u_ref[HALO:HALO + tm, :] = jnp.dot(h, w_ref[...], preferred_element_type=F32)
            carry_ref[j, which] = u_ref[tm:tm + HALO, :]

    def conv_gate(slot):
        def conv(u_ref, cw_ref, cb_ref, r):
            acc = cb_ref[...] + cw_ref[FFN_CONV - 1:FFN_CONV, :] * u_ref[HALO + r * rb:HALO + (r + 1) * rb, :]
            for t in range(FFN_CONV - 1):
                off = HALO + r * rb - (FFN_CONV - 1) + t
                acc = acc + cw_ref[t:t + 1, :] * u_ref[off:off + rb, :]
            return acc

        for r in range(tm // rb):
            act_ref[slot][r * rb:(r + 1) * rb, :] = (
                _silu(conv(ug_ref, cwg_ref, cbg_ref, r)) * conv(uv_ref, cwv_ref, cbv_ref, r)).astype(BF16)

    def down(slot):
        o_ref[...] += jnp.dot(act_ref[slot][...], wd_ref[...], preferred_element_type=F32)

    @pl.when(j == 0)
    def _():
        up()
        conv_gate(0)

    for parity in range(2):
        @pl.when((j > 0) & (j < n_slabs) & (j % 2 == parity))
        def _():
            up()
            down(1 - parity)
            conv_gate(parity)

    @pl.when(j == n_slabs)
    def _():
        down((n_slabs - 1) % 2)
        gate = gate_ref[...]
        fnw = fnw_ref[...]

        def body(r, carry):
            rows = pl.ds(pl.multiple_of(r * rb, rb), rb)
            y = x_ref[rows, :] + gate * o_ref[rows, :]
            o_ref[rows, :] = y * lax.rsqrt(jnp.mean(y * y, axis=-1, keepdims=True) + EPS) * fnw
            return carry

        lax.fori_loop(0, tm // rb, body, 0)


def _ffn_call(x1, norm_w, mod, final_w, w_up_bf, conv_w, conv_b, w_down_bf, tm, tf):
    s = x1.shape[0]
    n_slabs = D_FF // tf

    def vec(k):
        return pl.BlockSpec((1, D_MODEL), lambda i, j: (0, k))

    def up(j):
        return jnp.minimum(j, n_slabs - 1)

    def dn(j):
        return jnp.maximum(j - 1, 0)

    return pl.pallas_call(
        functools.partial(_ffn_kernel, tm=tm, n_slabs=n_slabs),
        grid=(s // tm, n_slabs + 1),
        in_specs=[pl.BlockSpec((tm, D_MODEL), lambda i, j: (i, 0)),
                  pl.BlockSpec((1, D_MODEL), lambda i, j: (0, 0)),
                  vec(4), vec(3), vec(5),
                  pl.BlockSpec((1, D_MODEL), lambda i, j: (0, 0)),
                  pl.BlockSpec((D_MODEL, tf), lambda i, j: (0, up(j))),
                  pl.BlockSpec((D_MODEL, tf), lambda i, j: (0, n_slabs + up(j))),
                  pl.BlockSpec((FFN_CONV, tf), lambda i, j: (0, up(j))),
                  pl.BlockSpec((FFN_CONV, tf), lambda i, j: (0, n_slabs + up(j))),
                  pl.BlockSpec((1, tf), lambda i, j: (0, up(j))),
                  pl.BlockSpec((1, tf), lambda i, j: (0, n_slabs + up(j))),
                  pl.BlockSpec((tf, D_MODEL), lambda i, j: (dn(j), 0))],
        out_specs=pl.BlockSpec((tm, D_MODEL), lambda i, j: (i, 0)),
        out_shape=jax.ShapeDtypeStruct((s, D_MODEL), F32),
        scratch_shapes=[pltpu.VMEM((tm, D_MODEL), BF16),
                        pltpu.VMEM((tm + HALO, tf), F32), pltpu.VMEM((tm + HALO, tf), F32),
                        pltpu.VMEM((tm, tf), BF16), pltpu.VMEM((tm, tf), BF16),
                        pltpu.VMEM((n_slabs, 2, HALO, tf), F32)],
        compiler_params=_cparams(2),
        name="ffn",
    )(x1, norm_w, mod, mod, mod, final_w, w_up_bf, w_up_bf, conv_w, conv_w, conv_b, conv_b, w_down_bf)


def _tri3(n):
    tri = jnp.tril(jnp.ones((n, n), F32)).astype(BF16)
    return jnp.concatenate([tri, tri, tri], axis=1)


def _expand_matrix(n_terms, width):
    src = jnp.arange(n_terms * LANES) % LANES
    dst = jnp.arange(SSM_HEADS * width) // width
    return (src[:, None] == dst[None, :]).astype(BF16)


def _shift_matrix(n):
    t = jnp.arange((SSM_CONV - 1) * n)
    src = n + t % n - (SSM_CONV - 1) + t // n
    return (src[:, None] == jnp.arange(2 * n)[None, :]).astype(BF16)


def _pad_lanes(v):
    return jnp.pad(v.reshape(1, -1), ((0, 0), (0, LANES - v.shape[-1])))


def kernel(x, c, w_mod, b_mod, norm1_w, w_in, hgrn_lb, hgrn_gnorm_w, ssd_conv_w, ssd_conv_b, ssd_dt_bias,
           ssd_a_log, ssd_d, ssd_norm_w, w_out, norm2_w, ffn_w_up, ffn_conv_w, ffn_conv_b, ffn_w_down,
           final_norm_w):
    bsz, seq, _ = x.shape
    assert bsz == 1 and w_in.shape[0] == 1, "single batch element, single layer"
    assert seq % CHUNK == 0
    x2 = x.reshape(seq, D_MODEL)
    tm_in = min(seq, 1024)
    tile_mix = min(seq, 256)
    tm_out = min(seq, 512)
    tm_ffn = min(seq, 1024)

    mod = _mod_call(c.reshape(D_MODEL, 1), w_mod[0], b_mod)

    w_in_bf = w_in[0].astype(BF16)
    w_dt = jnp.pad(w_in_bf[:, D_MAIN:], ((0, 0), (0, LANES - SSM_HEADS)))
    proj, dt_raw = _inproj_call(x2, norm1_w, mod, w_in_bf, w_dt, tm_in, D_MAIN // 4)

    tri3 = _tri3(CHUNK)
    o_a, w_up_bf = _hgrn_call(proj, hgrn_lb, hgrn_gnorm_w, tri3, ffn_w_up[0], tile_mix)
    o_b, w_down_bf, w_out_bf = _ssd_call(
        proj, dt_raw, ssd_conv_w[0], ssd_conv_b, _pad_lanes(ssd_dt_bias[0]), _pad_lanes(ssd_a_log[0]),
        jnp.repeat(ssd_d[0], SSM_P).reshape(1, D_SSM), ssd_norm_w, tri3, _expand_matrix(3, LANES),
        _expand_matrix(2, SSM_P), _shift_matrix(CHUNK), ffn_w_down[0], w_out[0], tile_mix)

    x1 = _outproj_call(x2, o_a, o_b, w_out_bf, mod, tm_out)

    out = _ffn_call(x1, norm2_w, mod, final_norm_w.reshape(1, D_MODEL), w_up_bf, ffn_conv_w[0], ffn_conv_b,
                    w_down_bf, tm_ffn, 512)
    return out.reshape(bsz, seq, D_MODEL)
```

```python
import functools
import math

import jax
import jax.numpy as jnp
from jax import lax
from jax.experimental import pallas as pl
from jax.experimental.pallas import tpu as pltpu

F32 = jnp.float32
BF16 = jnp.bfloat16

D_MODEL = 2048
D_HGRN = 1024
HGRN_HEADS = 8
HEAD_K = 128
D_SSM = 1024
SSM_HEADS = 16
SSM_P = 64
SSM_GROUPS = 4
SSM_N = 128
SSM_CONV = 4
D_MAIN = 4 * D_HGRN + D_SSM + (D_SSM + 2 * SSM_GROUPS * SSM_N)
D_FF = 5632
FFN_CONV = 3
EPS = 1e-6
LOG2E = math.log2(math.e)

LANES = 128
SUBLANES = 8
BF16_ROWS = 16
CHUNK = 128
FAST_LEVELS = 5
FAST_MAX_LOG2 = 100.0
HALO = SUBLANES
VMEM_LIMIT = 60 * 1024 * 1024

NT_DIMS = (((1,), (1,)), ((), ()))


def _cparams(n_axes, flags=None):
    return pltpu.CompilerParams(dimension_semantics=("arbitrary",) * n_axes,
                                vmem_limit_bytes=VMEM_LIMIT, flags=flags)


def _sigmoid(x):
    return 1.0 / (1.0 + jnp.exp(-x))


def _silu(x):
    return x * _sigmoid(x)


def _split_bf16(x, n_terms):
    terms = []
    r = x
    for t in range(n_terms):
        p = r.astype(BF16)
        terms.append(p)
        if t + 1 < n_terms:
            r = r - p.astype(F32)
    return terms


def _cumsum_rows(tri3, x):
    return jnp.dot(tri3, jnp.concatenate(_split_bf16(x, 3), axis=0), preferred_element_type=F32)


def _mod_kernel(c_ref, w_ref, b_ref, o_ref):
    s = _silu(c_ref[...])
    o_ref[...] = jnp.sum(w_ref[...] * s, axis=0, keepdims=True) + b_ref[...]


def _mod_call(c_col, w_mod, b_mod):
    d, n = w_mod.shape
    tn = 1024
    return pl.pallas_call(
        _mod_kernel,
        grid=(n // tn,),
        in_specs=[pl.BlockSpec((d, 1), lambda j: (0, 0)),
                  pl.BlockSpec((d, tn), lambda j: (0, j)),
                  pl.BlockSpec((1, tn), lambda j: (0, j))],
        out_specs=pl.BlockSpec((1, tn), lambda j: (0, j)),
        out_shape=jax.ShapeDtypeStruct((1, n), F32),
        compiler_params=_cparams(1),
        name="mod",
    )(c_col, w_mod, b_mod)


def _norm_mod_rows(x_ref, nw_ref, sc_ref, sh_ref, h_ref, n_rows, row_block=128):
    nw = nw_ref[...]
    sc = 1.0 + sc_ref[...]
    sh = sh_ref[...]

    def body(r, carry):
        rows = pl.ds(pl.multiple_of(r * row_block, row_block), row_block)
        x = x_ref[rows, :]
        y = x * lax.rsqrt(jnp.mean(x * x, axis=-1, keepdims=True) + EPS)
        h_ref[rows, :] = ((y * nw) * sc + sh).astype(BF16)
        return carry

    lax.fori_loop(0, n_rows // row_block, body, 0)


def _inproj_kernel(x_ref, nw_ref, sc_ref, sh_ref, w_ref, wdt_ref, o_ref, dt_ref, h_ref, *, tm):
    @pl.when(pl.program_id(1) == 0)
    def _():
        _norm_mod_rows(x_ref, nw_ref, sc_ref, sh_ref, h_ref, tm)
        dt_ref[...] = jnp.dot(h_ref[...], wdt_ref[...], preferred_element_type=F32)

    o_ref[...] = jnp.dot(h_ref[...], w_ref[...], preferred_element_type=F32).astype(BF16)


def _inproj_call(x2, norm_w, mod, w_bf, w_dt, tm, tn):
    s = x2.shape[0]
    return pl.pallas_call(
        functools.partial(_inproj_kernel, tm=tm),
        grid=(s // tm, D_MAIN // tn),
        in_specs=[pl.BlockSpec((tm, D_MODEL), lambda i, j: (i, 0)),
                  pl.BlockSpec((1, D_MODEL), lambda i, j: (0, 0)),
                  pl.BlockSpec((1, D_MODEL), lambda i, j: (0, 1)),
                  pl.BlockSpec((1, D_MODEL), lambda i, j: (0, 0)),
                  pl.BlockSpec((D_MODEL, tn), lambda i, j: (0, j)),
                  pl.BlockSpec((D_MODEL, LANES), lambda i, j: (0, 0))],
        out_specs=[pl.BlockSpec((tm, tn), lambda i, j: (i, j)),
                   pl.BlockSpec((tm, LANES), lambda i, j: (i, 0))],
        out_shape=[jax.ShapeDtypeStruct((s, D_MAIN), BF16),
                   jax.ShapeDtypeStruct((s, LANES), F32)],
        scratch_shapes=[pltpu.VMEM((tm, D_MODEL), BF16)],
        compiler_params=_cparams(2),
        name="inproj",
    )(x2, norm_w, mod, mod, w_bf, w_dt)


def _row_bcast(b_ref, bh, row, n_rows):
    return jnp.broadcast_to(b_ref[bh, row:row + 1, :], (n_rows, LANES))


def _hgrn_kernel(q_ref, f_ref, v_ref, g_ref, lb_ref, gnw_ref, tri_ref, wup_ref, o_ref, wup_bf_ref,
                 st_ref, b_ref, fg_ref, *, n_chunks):
    c_len = CHUNK
    n_levels = c_len.bit_length() - 1

    @pl.when(pl.program_id(0) == 0)
    def _():
        st_ref[...] = jnp.zeros_like(st_ref)

    wup_bf_ref[...] = wup_ref[...].astype(BF16)

    row2 = lax.broadcasted_iota(jnp.int32, (c_len, c_len), 0)
    col2 = lax.broadcasted_iota(jnp.int32, (c_len, c_len), 1)
    xor2 = row2 ^ col2
    level = jnp.where(row2 == col2, n_levels, -1)
    for lvl in range(n_levels):
        level = jnp.where(((xor2 >> lvl) == 1) & (row2 > col2), lvl, level)
    in_block = (level == n_levels) | ((level >= 0) & (level < FAST_LEVELS))
    rowk = lax.broadcasted_iota(jnp.int32, (c_len, LANES), 0)
    sub8 = lax.broadcasted_iota(jnp.int32, (SUBLANES, LANES), 0)
    tri3 = tri_ref[...]

    def block_anchor(bh, blk):
        size = 1 << FAST_LEVELS
        return _row_bcast(b_ref, bh, blk * size + size // 2 - 1, size)

    def gates(c):
        rows = slice(c * c_len, (c + 1) * c_len)
        worst = jnp.zeros((1 << FAST_LEVELS, LANES), F32)
        for h in range(HGRN_HEADS):
            cs = slice(h * HEAD_K, (h + 1) * HEAD_K)
            bh = c * HGRN_HEADS + h
            a0 = lb_ref[0:1, cs]
            a1 = lb_ref[1:2, cs]
            amax = jnp.maximum(a0, a1)
            e0 = jnp.exp(a0 - amax)
            lb = e0 / (e0 + jnp.exp(a1 - amax))
            fg = lb + (1.0 - lb) * _sigmoid(f_ref[rows, cs].astype(F32))
            fg_ref[bh] = fg
            b = _cumsum_rows(tri3, jnp.log(fg) * LOG2E)
            b_ref[bh] = b
            for blk in range(c_len >> FAST_LEVELS):
                span = slice(blk << FAST_LEVELS, (blk + 1) << FAST_LEVELS)
                worst = jnp.maximum(worst, jnp.abs(b[span] - block_anchor(bh, blk)))
        return jnp.max(worst) <= FAST_MAX_LOG2

    def heads(c, fast):
        rows = slice(c * c_len, (c + 1) * c_len)
        for h in range(HGRN_HEADS):
            cs = slice(h * HEAD_K, (h + 1) * HEAD_K)
            bh = c * HGRN_HEADS + h
            qs = _silu(q_ref[rows, cs].astype(F32))
            vb = v_ref[rows, cs]
            fg = fg_ref[bh]
            k = 1.0 - fg
            b = b_ref[bh]

            if fast:
                d = b - jnp.concatenate([block_anchor(bh, blk) for blk in range(c_len >> FAST_LEVELS)], axis=0)
                att = jnp.where(in_block,
                                lax.dot_general((qs * jnp.exp2(d)).astype(BF16), (k * jnp.exp2(-d)).astype(BF16),
                                                NT_DIMS, preferred_element_type=F32), 0.0)
            else:
                att = jnp.where(level == n_levels,
                                lax.dot_general(qs.astype(BF16), k.astype(BF16), NT_DIMS,
                                                preferred_element_type=F32), 0.0)
            for lvl in range(FAST_LEVELS if fast else 0, n_levels):
                m = 1 << lvl
                if m >= SUBLANES:
                    parts = []
                    for blk in range(c_len // (2 * m)):
                        lo = slice(blk * 2 * m, blk * 2 * m + m)
                        up = slice(blk * 2 * m + m, (blk + 1) * 2 * m)
                        bc = _row_bcast(b_ref, bh, blk * 2 * m + m - 1, m)
                        parts.append(k[lo] * jnp.exp2(bc - b[lo]))
                        parts.append(qs[up] * jnp.exp2(b[up] - bc))
                    xl = jnp.concatenate(parts, axis=0)
                else:
                    upper = ((rowk >> lvl) & 1) == 1
                    if lvl == 0:
                        xl = jnp.where(upper, qs * fg, k)
                    else:
                        tiles = []
                        for t8 in range(c_len // SUBLANES):
                            if lvl == 2:
                                tiles.append(_row_bcast(b_ref, bh, t8 * 8 + 3, SUBLANES))
                            else:
                                tiles.append(jnp.where(sub8 < 4, _row_bcast(b_ref, bh, t8 * 8 + 1, SUBLANES),
                                                       _row_bcast(b_ref, bh, t8 * 8 + 5, SUBLANES)))
                        d = b - jnp.concatenate(tiles, axis=0)
                        xl = jnp.where(upper, qs, k) * jnp.exp2(jnp.where(upper, d, -d))
                xb = xl.astype(BF16)
                att = jnp.where(level == lvl,
                                lax.dot_general(xb, xb, NT_DIMS, preferred_element_type=F32), att)

            st = st_ref[h]
            qc = (qs * jnp.exp2(b)).astype(BF16)
            o = jnp.dot(att.astype(BF16), vb, preferred_element_type=F32)
            o = o + lax.dot_general(qc, st.astype(BF16), NT_DIMS, preferred_element_type=F32)
            b_last = b_ref[bh, c_len - 1:c_len, :]
            ke = (k * jnp.exp2(b_last - b)).astype(BF16)
            vt = vb.astype(F32).T.astype(BF16)
            st_ref[h] = st * jnp.exp2(b_last) + jnp.dot(vt, ke, preferred_element_type=F32)

            on = o * lax.rsqrt(jnp.mean(o * o, axis=-1, keepdims=True) + EPS)
            o_ref[rows, cs] = (on * gnw_ref[0:1, cs] * _silu(g_ref[rows, cs].astype(F32))).astype(BF16)

    fast_ok = gates(0)
    for c in range(1, n_chunks):
        fast_ok = fast_ok & gates(c)

    @pl.when(fast_ok)
    def _():
        for c in range(n_chunks):
            heads(c, True)

    @pl.when(jnp.logical_not(fast_ok))
    def _():
        for c in range(n_chunks):
            heads(c, False)


def _hgrn_call(proj, hgrn_lb, gnorm_w, tri3, w_up, tile):
    s = proj.shape[0]
    n_steps = s // tile
    up_rows, up_cols = w_up.shape
    assert up_rows % (n_steps * BF16_ROWS) == 0

    def col(k):
        return pl.BlockSpec((tile, D_HGRN), lambda i: (i, k))

    return pl.pallas_call(
        functools.partial(_hgrn_kernel, n_chunks=tile // CHUNK),
        grid=(n_steps,),
        in_specs=[col(0), col(1), col(2), col(3),
                  pl.BlockSpec((2, D_HGRN), lambda i: (0, 0)),
                  pl.BlockSpec((1, D_HGRN), lambda i: (0, 0)),
                  pl.BlockSpec((CHUNK, 3 * CHUNK), lambda i: (0, 0)),
                  pl.BlockSpec((up_rows // n_steps, up_cols), lambda i: (i, 0))],
        out_specs=[pl.BlockSpec((tile, D_HGRN), lambda i: (i, 0)),
                   pl.BlockSpec((up_rows // n_steps, up_cols), lambda i: (i, 0))],
        out_shape=[jax.ShapeDtypeStruct((s, D_HGRN), BF16),
                   jax.ShapeDtypeStruct(w_up.shape, BF16)],
        scratch_shapes=[pltpu.VMEM((HGRN_HEADS, HEAD_K, HEAD_K), F32),
                        pltpu.VMEM((tile // CHUNK * HGRN_HEADS, CHUNK, LANES), F32),
                        pltpu.VMEM((tile // CHUNK * HGRN_HEADS, CHUNK, LANES), F32)],
        compiler_params=_cparams(1),
        name="hgrn",
    )(proj, proj, proj, proj, hgrn_lb, gnorm_w, tri3, w_up)


def _ssd_kernel(z_ref, xr_ref, bcr_ref, dt_ref, cwx_ref, cwbc_ref, cbx_ref, cbbc_ref, dtb_ref, alog_ref,
                dexp_ref, nw_ref, tri_ref, ecol_ref, e64_ref, shift_ref, wdn_ref, wout_ref,
                o_ref, wdn_bf_ref, wout_bf_ref,
                ht_ref, halo_x, halo_bc, cx_ref, cbc_ref, acol_all, ecol_all, ats_all, y_all, *, tile):
    c_len = CHUNK
    n_chunks = tile // c_len
    pair_w = 2 * SSM_P

    @pl.when(pl.program_id(0) == 0)
    def _():
        ht_ref[...] = jnp.zeros_like(ht_ref)
        halo_x[...] = jnp.zeros_like(halo_x)
        halo_bc[...] = jnp.zeros_like(halo_bc)

    wdn_bf_ref[...] = wdn_ref[...].astype(BF16)
    wout_bf_ref[...] = wout_ref[...].astype(BF16)

    slab = 512
    for raw_ref, halo, cw_ref, cb_ref, out in ((xr_ref, halo_x, cwx_ref, cbx_ref, cx_ref),
                                               (bcr_ref, halo_bc, cwbc_ref, cbbc_ref, cbc_ref)):
        for r in range(n_chunks):
            for s0 in range(0, D_SSM, slab):
                cols = slice(s0, s0 + slab)
                cur = raw_ref[r * c_len:(r + 1) * c_len, cols]
                prev = halo[:, cols] if r == 0 else raw_ref[(r - 1) * c_len:r * c_len, cols]
                sh = jnp.dot(shift_ref[...], jnp.concatenate([prev, cur], axis=0), preferred_element_type=F32)
                acc = cb_ref[:, cols] + cw_ref[SSM_CONV - 1:SSM_CONV, cols] * cur.astype(F32)
                for j in range(SSM_CONV - 1):
                    acc = acc + cw_ref[j:j + 1, cols] * sh[j * c_len:(j + 1) * c_len]
                out[r * c_len:(r + 1) * c_len, cols] = _silu(acc)
        halo[...] = raw_ref[tile - c_len:tile, :]

    lane = lax.broadcasted_iota(jnp.int32, (c_len, LANES), 1)
    row2 = lax.broadcasted_iota(jnp.int32, (c_len, c_len), 0)
    col2 = lax.broadcasted_iota(jnp.int32, (c_len, c_len), 1)
    causal = row2 >= col2
    first_head = lane < SSM_P
    head_lane = lax.broadcasted_iota(jnp.int32, (1, LANES), 1) < SSM_HEADS
    a_head = jnp.where(head_lane, -jnp.exp(alog_ref[...]) * LOG2E, 0.0)
    tri3 = tri_ref[...]

    def chunk(c, acol_s, ecol_s, ats_s, y_s):
        rows = slice(c * c_len, (c + 1) * c_len)
        dt = jax.nn.softplus(dt_ref[rows, :] + dtb_ref[...])
        acum = _cumsum_rows(tri3, dt * a_head)
        a_last = acum[c_len - 1:c_len, :]
        acol_s[...] = jnp.dot(jnp.concatenate(_split_bf16(acum, 3), axis=1), ecol_ref[...],
                              preferred_element_type=F32)
        ecol_s[...] = jnp.dot(jnp.concatenate(_split_bf16(jnp.exp2(acum), 2), axis=1),
                              ecol_ref[0:2 * LANES, :], preferred_element_type=F32)
        wdec = dt * jnp.exp2(a_last - acum)
        wexp = jnp.dot(jnp.concatenate(_split_bf16(wdec, 2), axis=1), e64_ref[...],
                       preferred_element_type=F32)
        ats_s[...] = (acum - jnp.log(dt) * LOG2E).T

        for g in range(SSM_GROUPS):
            bg = cbc_ref[rows, g * SSM_N:(g + 1) * SSM_N]
            cg = cbc_ref[rows, (SSM_GROUPS + g) * SSM_N:(SSM_GROUPS + g + 1) * SSM_N]
            cb = lax.dot_general(cg.astype(BF16), bg.astype(BF16), NT_DIMS, preferred_element_type=F32)
            bgt = bg.T.astype(BF16)
            for pp in range(2):
                pair = 2 * g + pp
                ha, hb = 2 * pair, 2 * pair + 1
                ps = slice(pair * pair_w, (pair + 1) * pair_w)
                xp = cx_ref[rows, ps]
                xpb = xp.astype(BF16)
                zero = jnp.zeros_like(xpb)
                y = None
                for hh, keep in ((ha, first_head), (hb, ~first_head)):
                    ex = acol_s[:, hh * LANES:(hh + 1) * LANES] - ats_s[hh:hh + 1, :]
                    m = (cb * jnp.exp2(jnp.where(causal, ex, -jnp.inf))).astype(BF16)
                    yh = jnp.dot(m, jnp.where(keep, xpb, zero), preferred_element_type=F32)
                    y = yh if y is None else y + yh
                ce = jnp.concatenate([(cg * ecol_s[:, ha * LANES:(ha + 1) * LANES]).astype(BF16),
                                      (cg * ecol_s[:, hb * LANES:(hb + 1) * LANES]).astype(BF16)], axis=1)
                ht = ht_ref[pair]
                y = y + jnp.dot(ce, ht.astype(BF16), preferred_element_type=F32)

                xw = (xp * wexp[:, ps]).astype(BF16)
                dca = ecol_s[c_len - 1:c_len, ha * LANES:(ha + 1) * LANES]
                dcb = ecol_s[c_len - 1:c_len, hb * LANES:(hb + 1) * LANES]
                ht_ref[pair, 0:SSM_N, :] = ht[0:SSM_N] * dca + jnp.dot(
                    bgt, jnp.where(first_head, xw, zero), preferred_element_type=F32)
                ht_ref[pair, SSM_N:2 * SSM_N, :] = ht[SSM_N:] * dcb + jnp.dot(
                    bgt, jnp.where(first_head, zero, xw), preferred_element_type=F32)

                y = y + dexp_ref[0:1, ps] * xp
                y_s[:, ps] = y * _silu(z_ref[rows, ps].astype(F32))

            gs = slice(g * 2 * pair_w, (g + 1) * 2 * pair_w)
            yg = y_s[:, gs]
            yn = yg * lax.rsqrt(jnp.mean(yg * yg, axis=-1, keepdims=True) + EPS)
            o_ref[rows, gs] = (yn * nw_ref[0:1, gs]).astype(BF16)

    for c in range(n_chunks):
        chunk(c, acol_all.at[c], ecol_all.at[c], ats_all.at[c], y_all.at[c])


def _ssd_call(proj, dt_raw, conv_w, conv_b, dt_bias, a_log, d_exp, norm_w, tri3, e_col, e_64, shift3,
              w_down, w_out, tile):
    s = proj.shape[0]
    n_steps = s // tile
    dn_col_blocks = 2 if n_steps % 2 == 0 else 1
    dn_rows = w_down.shape[0] // (n_steps // dn_col_blocks)
    dn_cols = w_down.shape[1] // dn_col_blocks
    out_rows = w_out.shape[0] // n_steps
    assert dn_rows % BF16_ROWS == 0 and out_rows % BF16_ROWS == 0

    def col(k):
        return pl.BlockSpec((tile, D_SSM), lambda i: (i, k))

    def full(shape):
        return pl.BlockSpec(shape, lambda i: (0, 0))

    dn_spec = pl.BlockSpec((dn_rows, dn_cols), lambda i: (i // dn_col_blocks, i % dn_col_blocks))
    out_spec = pl.BlockSpec((out_rows, w_out.shape[1]), lambda i: (i, 0))
    n_pairs = SSM_HEADS // 2
    return pl.pallas_call(
        functools.partial(_ssd_kernel, tile=tile),
        grid=(n_steps,),
        in_specs=[col(4), col(5), col(6),
                  pl.BlockSpec((tile, LANES), lambda i: (i, 0)),
                  pl.BlockSpec((SSM_CONV, D_SSM), lambda i: (0, 0)),
                  pl.BlockSpec((SSM_CONV, D_SSM), lambda i: (0, 1)),
                  pl.BlockSpec((1, D_SSM), lambda i: (0, 0)),
                  pl.BlockSpec((1, D_SSM), lambda i: (0, 1)),
                  full((1, LANES)), full((1, LANES)), full((1, D_SSM)), full((1, D_SSM)),
                  full((CHUNK, 3 * CHUNK)), full(e_col.shape), full(e_64.shape), full(shift3.shape),
                  dn_spec, out_spec],
        out_specs=[pl.BlockSpec((tile, D_SSM), lambda i: (i, 0)), dn_spec, out_spec],
        out_shape=[jax.ShapeDtypeStruct((s, D_SSM), BF16),
                   jax.ShapeDtypeStruct(w_down.shape, BF16),
                   jax.ShapeDtypeStruct(w_out.shape, BF16)],
        scratch_shapes=[pltpu.VMEM((n_pairs, 2 * SSM_N, LANES), F32),
                        pltpu.VMEM((CHUNK, D_SSM), BF16), pltpu.VMEM((CHUNK, D_SSM), BF16),
                        pltpu.VMEM((tile, D_SSM), F32), pltpu.VMEM((tile, D_SSM), F32),
                        pltpu.VMEM((tile // CHUNK, CHUNK, SSM_HEADS * LANES), F32),
                        pltpu.VMEM((tile // CHUNK, CHUNK, SSM_HEADS * LANES), F32),
                        pltpu.VMEM((tile // CHUNK, CHUNK, LANES), F32),
                        pltpu.VMEM((tile // CHUNK, CHUNK, D_SSM), F32)],
        compiler_params=_cparams(1),
        name="ssd",
    )(proj, proj, proj, dt_raw, conv_w, conv_w, conv_b, conv_b, dt_bias, a_log, d_exp, norm_w, tri3,
      e_col, e_64, shift3, w_down, w_out)


def _outproj_kernel(x_ref, oa_ref, ob_ref, wa_ref, wb_ref, g_ref, o_ref):
    mixed = jnp.dot(oa_ref[...], wa_ref[...], preferred_element_type=F32)
    mixed = mixed + jnp.dot(ob_ref[...], wb_ref[...], preferred_element_type=F32)
    o_ref[...] = x_ref[...] + g_ref[...] * mixed


def _outproj_call(x2, o_a, o_b, w_out_bf, mod, tm):
    s = x2.shape[0]
    return pl.pallas_call(
        _outproj_kernel,
        grid=(s // tm,),
        in_specs=[pl.BlockSpec((tm, D_MODEL), lambda i: (i, 0)),
                  pl.BlockSpec((tm, D_HGRN), lambda i: (i, 0)),
                  pl.BlockSpec((tm, D_SSM), lambda i: (i, 0)),
                  pl.BlockSpec((D_HGRN, D_MODEL), lambda i: (0, 0)),
                  pl.BlockSpec((D_SSM, D_MODEL), lambda i: (1, 0)),
                  pl.BlockSpec((1, D_MODEL), lambda i: (0, 2))],
        out_specs=pl.BlockSpec((tm, D_MODEL), lambda i: (i, 0)),
        out_shape=jax.ShapeDtypeStruct((s, D_MODEL), F32),
        compiler_params=_cparams(1),
        name="outproj",
    )(x2, o_a, o_b, w_out_bf, w_out_bf, mod)


def _ffn_kernel(x_ref, nw_ref, sc_ref, sh_ref, gate_ref, fnw_ref, wug_ref, wuv_ref, cwg_ref, cwv_ref,
                cbg_ref, cbv_ref, wd_ref, o_ref, h_ref, ug_ref, uv_ref, act0_ref, act1_ref, carry_ref,
                *, tm, n_slabs):
    i = pl.program_id(0)
    j = pl.program_id(1)
    rb = 128
    act_ref = (act0_ref, act1_ref)

    @pl.when(j == 0)
    def _():
        _norm_mod_rows(x_ref, nw_ref, sc_ref, sh_ref, h_ref, tm)
        o_ref[...] = jnp.zeros_like(o_ref)

    @pl.when((i == 0) & (j < n_slabs))
    def _():
        carry_ref[j] = jnp.zeros(carry_ref.shape[1:], F32)

    def up_conv(slot):
        h = h_ref[...]
        for u_ref, w_ref, which in ((ug_ref, wug_ref, 0), (uv_ref, wuv_ref, 1)):
            u_ref[0:HALO, :] = carry_ref[j, which]
            u_ref[HALO:HALO + tm, :] = jnp.dot(h, w_ref[...], preferred_element_type=F32)
            carry_ref[j, which] = u_ref[tm:tm + HALO, :]

        def conv(u_ref, cw_ref, cb_ref, r):
            acc = cb_ref[...] + cw_ref[FFN_CONV - 1:FFN_CONV, :] * u_ref[HALO + r * rb:HALO + (r + 1) * rb, :]
            for t in range(FFN_CONV - 1):
                off = HALO + r * rb - (FFN_CONV - 1) + t
                acc = acc + cw_ref[t:t + 1, :] * u_ref[off:off + rb, :]
            return acc

        for r in range(tm // rb):
            act_ref[slot][r * rb:(r + 1) * rb, :] = (
                _silu(conv(ug_ref, cwg_ref, cbg_ref, r)) * conv(uv_ref, cwv_ref, cbv_ref, r)).astype(BF16)

    def down(slot):
        o_ref[...] += jnp.dot(act_ref[slot][...], wd_ref[...], preferred_element_type=F32)

    @pl.when(j == 0)
    def _():
        up_conv(0)

    for parity in range(2):
        @pl.when((j > 0) & (j < n_slabs) & (j % 2 == parity))
        def _():
            down(1 - parity)
            up_conv(parity)

    @pl.when(j == n_slabs)
    def _():
        down((n_slabs - 1) % 2)
        gate = gate_ref[...]
        fnw = fnw_ref[...]

        def body(r, carry):
            rows = pl.ds(pl.multiple_of(r * rb, rb), rb)
            y = x_ref[rows, :] + gate * o_ref[rows, :]
            o_ref[rows, :] = y * lax.rsqrt(jnp.mean(y * y, axis=-1, keepdims=True) + EPS) * fnw
            return carry

        lax.fori_loop(0, tm // rb, body, 0)


def _ffn_call(x1, norm_w, mod, final_w, w_up_bf, conv_w, conv_b, w_down_bf, tm, tf):
    s = x1.shape[0]
    n_slabs = D_FF // tf

    def vec(k):
        return pl.BlockSpec((1, D_MODEL), lambda i, j: (0, k))

    def up(j):
        return jnp.minimum(j, n_slabs - 1)

    def dn(j):
        return jnp.maximum(j - 1, 0)

    return pl.pallas_call(
        functools.partial(_ffn_kernel, tm=tm, n_slabs=n_slabs),
        grid=(s // tm, n_slabs + 1),
        in_specs=[pl.BlockSpec((tm, D_MODEL), lambda i, j: (i, 0)),
                  pl.BlockSpec((1, D_MODEL), lambda i, j: (0, 0)),
                  vec(4), vec(3), vec(5),
                  pl.BlockSpec((1, D_MODEL), lambda i, j: (0, 0)),
                  pl.BlockSpec((D_MODEL, tf), lambda i, j: (0, up(j))),
                  pl.BlockSpec((D_MODEL, tf), lambda i, j: (0, n_slabs + up(j))),
                  pl.BlockSpec((FFN_CONV, tf), lambda i, j: (0, up(j))),
                  pl.BlockSpec((FFN_CONV, tf), lambda i, j: (0, n_slabs + up(j))),
                  pl.BlockSpec((1, tf), lambda i, j: (0, up(j))),
                  pl.BlockSpec((1, tf), lambda i, j: (0, n_slabs + up(j))),
                  pl.BlockSpec((tf, D_MODEL), lambda i, j: (dn(j), 0))],
        out_specs=pl.BlockSpec((tm, D_MODEL), lambda i, j: (i, 0)),
        out_shape=jax.ShapeDtypeStruct((s, D_MODEL), F32),
        scratch_shapes=[pltpu.VMEM((tm, D_MODEL), BF16),
                        pltpu.VMEM((tm + HALO, tf), F32), pltpu.VMEM((tm + HALO, tf), F32),
                        pltpu.VMEM((tm, tf), BF16), pltpu.VMEM((tm, tf), BF16),
                        pltpu.VMEM((n_slabs, 2, HALO, tf), F32)],
        compiler_params=_cparams(2),
        name="ffn",
    )(x1, norm_w, mod, mod, mod, final_w, w_up_bf, w_up_bf, conv_w, conv_w, conv_b, conv_b, w_down_bf)


def _tri3(n):
    tri = jnp.tril(jnp.ones((n, n), F32)).astype(BF16)
    return jnp.concatenate([tri, tri, tri], axis=1)


def _expand_matrix(n_terms, width):
    src = jnp.arange(n_terms * LANES) % LANES
    dst = jnp.arange(SSM_HEADS * width) // width
    return (src[:, None] == dst[None, :]).astype(BF16)


def _shift_matrix(n):
    t = jnp.arange((SSM_CONV - 1) * n)
    src = n + t % n - (SSM_CONV - 1) + t // n
    return (src[:, None] == jnp.arange(2 * n)[None, :]).astype(BF16)


def _pad_lanes(v):
    return jnp.pad(v.reshape(1, -1), ((0, 0), (0, LANES - v.shape[-1])))


def kernel(x, c, w_mod, b_mod, norm1_w, w_in, hgrn_lb, hgrn_gnorm_w, ssd_conv_w, ssd_conv_b, ssd_dt_bias,
           ssd_a_log, ssd_d, ssd_norm_w, w_out, norm2_w, ffn_w_up, ffn_conv_w, ffn_conv_b, ffn_w_down,
           final_norm_w):
    bsz, seq, _ = x.shape
    assert bsz == 1 and w_in.shape[0] == 1, "single batch element, single layer"
    assert seq % CHUNK == 0
    x2 = x.reshape(seq, D_MODEL)
    tm_in = min(seq, 1024)
    tile_mix = min(seq, 256)
    tile_ssd = min(seq, 512)
    tm_out = min(seq, 512)
    tm_ffn = min(seq, 1024)

    mod = _mod_call(c.reshape(D_MODEL, 1), w_mod[0], b_mod)

    w_in_bf = w_in[0].astype(BF16)
    w_dt = jnp.pad(w_in_bf[:, D_MAIN:], ((0, 0), (0, LANES - SSM_HEADS)))
    proj, dt_raw = _inproj_call(x2, norm1_w, mod, w_in_bf, w_dt, tm_in, D_MAIN // 4)

    tri3 = _tri3(CHUNK)
    o_a, w_up_bf = _hgrn_call(proj, hgrn_lb, hgrn_gnorm_w, tri3, ffn_w_up[0], tile_mix)
    o_b, w_down_bf, w_out_bf = _ssd_call(
        proj, dt_raw, ssd_conv_w[0], ssd_conv_b, _pad_lanes(ssd_dt_bias[0]), _pad_lanes(ssd_a_log[0]),
        jnp.repeat(ssd_d[0], SSM_P).reshape(1, D_SSM), ssd_norm_w, tri3, _expand_matrix(3, LANES),
        _expand_matrix(2, SSM_P), _shift_matrix(CHUNK), ffn_w_down[0], w_out[0], tile_ssd)

    x1 = _outproj_call(x2, o_a, o_b, w_out_bf, mod, tm_out)

    out = _ffn_call(x1, norm2_w, mod, final_norm_w.reshape(1, D_MODEL), w_up_bf, ffn_conv_w[0], ffn_conv_b,
                    w_down_bf, tm_ffn, 512)
    return out.reshape(bsz, seq, D_MODEL)
```

```python
import functools
import math

import jax
import jax.numpy as jnp
from jax import lax
from jax.experimental import pallas as pl
from jax.experimental.pallas import tpu as pltpu

F32 = jnp.float32
BF16 = jnp.bfloat16

D_MODEL = 2048
D_HGRN = 1024
HGRN_HEADS = 8
HEAD_K = 128
D_SSM = 1024
SSM_HEADS = 16
SSM_P = 64
SSM_GROUPS = 4
SSM_N = 128
SSM_CONV = 4
D_MAIN = 4 * D_HGRN + D_SSM + (D_SSM + 2 * SSM_GROUPS * SSM_N)
D_FF = 5632
FFN_CONV = 3
EPS = 1e-6
LOG2E = math.log2(math.e)

LANES = 128
SUBLANES = 8
BF16_ROWS = 16
CHUNK = 128
CUMSUM_TERMS = 2
FAST_LEVELS = 5
FAST_MAX_LOG2 = 100.0
HALO = SUBLANES
VMEM_LIMIT = 60 * 1024 * 1024

NT_DIMS = (((1,), (1,)), ((), ()))


def _cparams(n_axes):
    return pltpu.CompilerParams(dimension_semantics=("arbitrary",) * n_axes, vmem_limit_bytes=VMEM_LIMIT)


def _sigmoid(x):
    return 1.0 / (1.0 + jnp.exp(-x))


def _silu(x):
    return x * _sigmoid(x)


def _split_bf16(x, n_terms):
    terms = []
    r = x
    for t in range(n_terms):
        p = r.astype(BF16)
        terms.append(p)
        if t + 1 < n_terms:
            r = r - p.astype(F32)
    return terms


def _cumsum_rows(tri3, x):
    c_len = x.shape[0]
    return jnp.dot(tri3[:, :CUMSUM_TERMS * c_len], jnp.concatenate(_split_bf16(x, CUMSUM_TERMS), axis=0),
                   preferred_element_type=F32)


def _mod_kernel(c_ref, w_ref, b_ref, o_ref):
    s = _silu(c_ref[...])
    o_ref[...] = jnp.sum(w_ref[...] * s, axis=0, keepdims=True) + b_ref[...]


def _mod_call(c_col, w_mod, b_mod):
    d, n = w_mod.shape
    tn = 1024
    return pl.pallas_call(
        _mod_kernel,
        grid=(n // tn,),
        in_specs=[pl.BlockSpec((d, 1), lambda j: (0, 0)),
                  pl.BlockSpec((d, tn), lambda j: (0, j)),
                  pl.BlockSpec((1, tn), lambda j: (0, j))],
        out_specs=pl.BlockSpec((1, tn), lambda j: (0, j)),
        out_shape=jax.ShapeDtypeStruct((1, n), F32),
        compiler_params=_cparams(1),
        name="mod",
    )(c_col, w_mod, b_mod)


def _norm_mod_rows(x_ref, nw_ref, sc_ref, sh_ref, h_ref, n_rows, row_block=128):
    w = nw_ref[...] * (1.0 + sc_ref[...])
    sh = sh_ref[...]

    def body(r, carry):
        rows = pl.ds(pl.multiple_of(r * row_block, row_block), row_block)
        x = x_ref[rows, :]
        y = x * lax.rsqrt(jnp.mean(x * x, axis=-1, keepdims=True) + EPS)
        h_ref[rows, :] = (y * w + sh).astype(BF16)
        return carry

    lax.fori_loop(0, n_rows // row_block, body, 0)


def _inproj_kernel(x_ref, nw_ref, sc_ref, sh_ref, w_ref, wdt_ref, o_ref, dt_ref, h_ref, *, tm):
    @pl.when(pl.program_id(1) == 0)
    def _():
        _norm_mod_rows(x_ref, nw_ref, sc_ref, sh_ref, h_ref, tm)
        dt_ref[...] = jnp.dot(h_ref[...], wdt_ref[...], preferred_element_type=F32)

    o_ref[...] = jnp.dot(h_ref[...], w_ref[...], preferred_element_type=F32).astype(BF16)


def _inproj_call(x2, norm_w, mod, w_bf, w_dt, tm, tn):
    s = x2.shape[0]
    return pl.pallas_call(
        functools.partial(_inproj_kernel, tm=tm),
        grid=(s // tm, D_MAIN // tn),
        in_specs=[pl.BlockSpec((tm, D_MODEL), lambda i, j: (i, 0)),
                  pl.BlockSpec((1, D_MODEL), lambda i, j: (0, 0)),
                  pl.BlockSpec((1, D_MODEL), lambda i, j: (0, 1)),
                  pl.BlockSpec((1, D_MODEL), lambda i, j: (0, 0)),
                  pl.BlockSpec((D_MODEL, tn), lambda i, j: (0, j)),
                  pl.BlockSpec((D_MODEL, LANES), lambda i, j: (0, 0))],
        out_specs=[pl.BlockSpec((tm, tn), lambda i, j: (i, j)),
                   pl.BlockSpec((tm, LANES), lambda i, j: (i, 0))],
        out_shape=[jax.ShapeDtypeStruct((s, D_MAIN), BF16),
                   jax.ShapeDtypeStruct((s, LANES), F32)],
        scratch_shapes=[pltpu.VMEM((tm, D_MODEL), BF16)],
        compiler_params=_cparams(2),
        name="inproj",
    )(x2, norm_w, mod, mod, w_bf, w_dt)


def _row_bcast(b_ref, bh, row, n_rows):
    return jnp.broadcast_to(b_ref[bh, row:row + 1, :], (n_rows, LANES))


def _hgrn_kernel(q_ref, f_ref, v_ref, g_ref, lb_ref, gnw_ref, tri_ref, wup_ref, o_ref, wup_bf_ref,
                 st_ref, b_ref, fg_ref, *, n_chunks):
    c_len = CHUNK
    n_levels = c_len.bit_length() - 1

    @pl.when(pl.program_id(0) == 0)
    def _():
        st_ref[...] = jnp.zeros_like(st_ref)

    wup_bf_ref[...] = wup_ref[...].astype(BF16)

    row2 = lax.broadcasted_iota(jnp.int32, (c_len, c_len), 0)
    col2 = lax.broadcasted_iota(jnp.int32, (c_len, c_len), 1)
    xor2 = row2 ^ col2
    level = jnp.where(row2 == col2, n_levels, -1)
    for lvl in range(n_levels):
        level = jnp.where(((xor2 >> lvl) == 1) & (row2 > col2), lvl, level)
    in_block = (level == n_levels) | ((level >= 0) & (level < FAST_LEVELS))
    rowk = lax.broadcasted_iota(jnp.int32, (c_len, LANES), 0)
    sub8 = lax.broadcasted_iota(jnp.int32, (SUBLANES, LANES), 0)
    tri3 = tri_ref[...]

    def block_anchor(bh, blk):
        size = 1 << FAST_LEVELS
        return _row_bcast(b_ref, bh, blk * size + size // 2 - 1, size)

    def gates(c):
        rows = slice(c * c_len, (c + 1) * c_len)
        worst = jnp.zeros((1 << FAST_LEVELS, LANES), F32)
        for h in range(HGRN_HEADS):
            cs = slice(h * HEAD_K, (h + 1) * HEAD_K)
            bh = c * HGRN_HEADS + h
            a0 = lb_ref[0:1, cs]
            a1 = lb_ref[1:2, cs]
            amax = jnp.maximum(a0, a1)
            e0 = jnp.exp(a0 - amax)
            lb = e0 / (e0 + jnp.exp(a1 - amax))
            fg = lb + (1.0 - lb) * _sigmoid(f_ref[rows, cs].astype(F32))
            fg_ref[bh] = fg
            b = _cumsum_rows(tri3, jnp.log(fg) * LOG2E)
            b_ref[bh] = b
            for blk in range(c_len >> FAST_LEVELS):
                span = slice(blk << FAST_LEVELS, (blk + 1) << FAST_LEVELS)
                worst = jnp.maximum(worst, jnp.abs(b[span] - block_anchor(bh, blk)))
        return jnp.max(worst) <= FAST_MAX_LOG2

    def heads(c, fast):
        rows = slice(c * c_len, (c + 1) * c_len)
        for h in range(HGRN_HEADS):
            cs = slice(h * HEAD_K, (h + 1) * HEAD_K)
            bh = c * HGRN_HEADS + h
            qs = _silu(q_ref[rows, cs].astype(F32))
            vb = v_ref[rows, cs]
            fg = fg_ref[bh]
            k = 1.0 - fg
            b = b_ref[bh]

            if fast:
                d = b - jnp.concatenate([block_anchor(bh, blk) for blk in range(c_len >> FAST_LEVELS)], axis=0)
                att = jnp.where(in_block,
                                lax.dot_general((qs * jnp.exp2(d)).astype(BF16), (k * jnp.exp2(-d)).astype(BF16),
                                                NT_DIMS, preferred_element_type=F32), 0.0)
            else:
                att = jnp.where(level == n_levels,
                                lax.dot_general(qs.astype(BF16), k.astype(BF16), NT_DIMS,
                                                preferred_element_type=F32), 0.0)
            for lvl in range(FAST_LEVELS if fast else 0, n_levels):
                m = 1 << lvl
                if m >= SUBLANES:
                    parts = []
                    for blk in range(c_len // (2 * m)):
                        lo = slice(blk * 2 * m, blk * 2 * m + m)
                        up = slice(blk * 2 * m + m, (blk + 1) * 2 * m)
                        bc = _row_bcast(b_ref, bh, blk * 2 * m + m - 1, m)
                        parts.append(k[lo] * jnp.exp2(bc - b[lo]))
                        parts.append(qs[up] * jnp.exp2(b[up] - bc))
                    xl = jnp.concatenate(parts, axis=0)
                else:
                    upper = ((rowk >> lvl) & 1) == 1
                    if lvl == 0:
                        xl = jnp.where(upper, qs * fg, k)
                    else:
                        tiles = []
                        for t8 in range(c_len // SUBLANES):
                            if lvl == 2:
                                tiles.append(_row_bcast(b_ref, bh, t8 * 8 + 3, SUBLANES))
                            else:
                                tiles.append(jnp.where(sub8 < 4, _row_bcast(b_ref, bh, t8 * 8 + 1, SUBLANES),
                                                       _row_bcast(b_ref, bh, t8 * 8 + 5, SUBLANES)))
                        d = b - jnp.concatenate(tiles, axis=0)
                        xl = jnp.where(upper, qs, k) * jnp.exp2(jnp.where(upper, d, -d))
                xb = xl.astype(BF16)
                att = jnp.where(level == lvl,
                                lax.dot_general(xb, xb, NT_DIMS, preferred_element_type=F32), att)

            st = st_ref[h]
            qc = (qs * jnp.exp2(b)).astype(BF16)
            o = jnp.dot(att.astype(BF16), vb, preferred_element_type=F32)
            o = o + lax.dot_general(qc, st.astype(BF16), NT_DIMS, preferred_element_type=F32)
            b_last = b_ref[bh, c_len - 1:c_len, :]
            ke = (k * jnp.exp2(b_last - b)).astype(BF16)
            vt = vb.T
            st_ref[h] = st * jnp.exp2(b_last) + jnp.dot(vt, ke, preferred_element_type=F32)

            on = o * lax.rsqrt(jnp.mean(o * o, axis=-1, keepdims=True) + EPS)
            o_ref[rows, cs] = (on * gnw_ref[0:1, cs] * _silu(g_ref[rows, cs].astype(F32))).astype(BF16)

    fast_ok = gates(0)
    for c in range(1, n_chunks):
        fast_ok = fast_ok & gates(c)

    @pl.when(fast_ok)
    def _():
        for c in range(n_chunks):
            heads(c, True)

    @pl.when(jnp.logical_not(fast_ok))
    def _():
        for c in range(n_chunks):
            heads(c, False)


def _hgrn_call(proj, hgrn_lb, gnorm_w, tri3, w_up, tile):
    s = proj.shape[0]
    n_steps = s // tile
    up_rows, up_cols = w_up.shape
    assert up_rows % (n_steps * BF16_ROWS) == 0

    def col(k):
        return pl.BlockSpec((tile, D_HGRN), lambda i: (i, k))

    return pl.pallas_call(
        functools.partial(_hgrn_kernel, n_chunks=tile // CHUNK),
        grid=(n_steps,),
        in_specs=[col(0), col(1), col(2), col(3),
                  pl.BlockSpec((2, D_HGRN), lambda i: (0, 0)),
                  pl.BlockSpec((1, D_HGRN), lambda i: (0, 0)),
                  pl.BlockSpec((CHUNK, 3 * CHUNK), lambda i: (0, 0)),
                  pl.BlockSpec((up_rows // n_steps, up_cols), lambda i: (i, 0))],
        out_specs=[pl.BlockSpec((tile, D_HGRN), lambda i: (i, 0)),
                   pl.BlockSpec((up_rows // n_steps, up_cols), lambda i: (i, 0))],
        out_shape=[jax.ShapeDtypeStruct((s, D_HGRN), BF16),
                   jax.ShapeDtypeStruct(w_up.shape, BF16)],
        scratch_shapes=[pltpu.VMEM((HGRN_HEADS, HEAD_K, HEAD_K), F32),
                        pltpu.VMEM((tile // CHUNK * HGRN_HEADS, CHUNK, LANES), F32),
                        pltpu.VMEM((tile // CHUNK * HGRN_HEADS, CHUNK, LANES), F32)],
        compiler_params=_cparams(1),
        name="hgrn",
    )(proj, proj, proj, proj, hgrn_lb, gnorm_w, tri3, w_up)


def _ssd_kernel(z_ref, xr_ref, bcr_ref, dt_ref, cwx_ref, cwbc_ref, cbx_ref, cbbc_ref, dtb_ref, alog_ref,
                dexp_ref, nw_ref, tri_ref, ecol_ref, e64_ref, shift_ref, wdn_ref, wout_ref,
                o_ref, wdn_bf_ref, wout_bf_ref,
                ht_ref, halo_x, halo_bc, cx_ref, cbc_ref, acol_all, ecol_all, ats_all, y_all, *, tile):
    c_len = CHUNK
    n_chunks = tile // c_len
    pair_w = 2 * SSM_P

    @pl.when(pl.program_id(0) == 0)
    def _():
        ht_ref[...] = jnp.zeros_like(ht_ref)
        halo_x[...] = jnp.zeros_like(halo_x)
        halo_bc[...] = jnp.zeros_like(halo_bc)

    wdn_bf_ref[...] = wdn_ref[...].astype(BF16)
    wout_bf_ref[...] = wout_ref[...].astype(BF16)

    slab = 512
    for raw_ref, halo, cw_ref, cb_ref, out in ((xr_ref, halo_x, cwx_ref, cbx_ref, cx_ref),
                                               (bcr_ref, halo_bc, cwbc_ref, cbbc_ref, cbc_ref)):
        for r in range(n_chunks):
            for s0 in range(0, D_SSM, slab):
                cols = slice(s0, s0 + slab)
                cur = raw_ref[r * c_len:(r + 1) * c_len, cols]
                prev = halo[:, cols] if r == 0 else raw_ref[(r - 1) * c_len:r * c_len, cols]
                sh = jnp.dot(shift_ref[...], jnp.concatenate([prev, cur], axis=0), preferred_element_type=F32)
                acc = cb_ref[:, cols] + cw_ref[SSM_CONV - 1:SSM_CONV, cols] * cur.astype(F32)
                for j in range(SSM_CONV - 1):
                    acc = acc + cw_ref[j:j + 1, cols] * sh[j * c_len:(j + 1) * c_len]
                out[r * c_len:(r + 1) * c_len, cols] = _silu(acc)
        halo[...] = raw_ref[tile - c_len:tile, :]

    lane = lax.broadcasted_iota(jnp.int32, (c_len, LANES), 1)
    row2 = lax.broadcasted_iota(jnp.int32, (c_len, c_len), 0)
    col2 = lax.broadcasted_iota(jnp.int32, (c_len, c_len), 1)
    causal = row2 >= col2
    first_head = lane < SSM_P
    head_lane = lax.broadcasted_iota(jnp.int32, (1, LANES), 1) < SSM_HEADS
    a_head = jnp.where(head_lane, -jnp.exp(alog_ref[...]) * LOG2E, 0.0)
    tri3 = tri_ref[...]

    def chunk(c, acol_s, ecol_s, ats_s, y_s):
        rows = slice(c * c_len, (c + 1) * c_len)
        dt = jax.nn.softplus(dt_ref[rows, :] + dtb_ref[...])
        acum = _cumsum_rows(tri3, dt * a_head)
        a_last = acum[c_len - 1:c_len, :]
        acol_s[...] = jnp.dot(jnp.concatenate(_split_bf16(acum, 3), axis=1), ecol_ref[...],
                              preferred_element_type=F32)
        ecol_s[...] = jnp.dot(jnp.concatenate(_split_bf16(jnp.exp2(acum), 2), axis=1),
                              ecol_ref[0:2 * LANES, :], preferred_element_type=F32)
        wdec = dt * jnp.exp2(a_last - acum)
        wexp = jnp.dot(jnp.concatenate(_split_bf16(wdec, 2), axis=1), e64_ref[...],
                       preferred_element_type=F32)
        ats_s[...] = (acum - jnp.log(dt) * LOG2E).T

        for g in range(SSM_GROUPS):
            bg = cbc_ref[rows, g * SSM_N:(g + 1) * SSM_N]
            cg = cbc_ref[rows, (SSM_GROUPS + g) * SSM_N:(SSM_GROUPS + g + 1) * SSM_N]
            cb = lax.dot_general(cg.astype(BF16), bg.astype(BF16), NT_DIMS, preferred_element_type=F32)
            bgt = bg.T.astype(BF16)
            for pp in range(2):
                pair = 2 * g + pp
                ha, hb = 2 * pair, 2 * pair + 1
                ps = slice(pair * pair_w, (pair + 1) * pair_w)
                xp = cx_ref[rows, ps]
                xpb = xp.astype(BF16)
                zero = jnp.zeros_like(xpb)
                y = None
                for hh, keep in ((ha, first_head), (hb, ~first_head)):
                    ex = acol_s[:, hh * LANES:(hh + 1) * LANES] - ats_s[hh:hh + 1, :]
                    m = (cb * jnp.exp2(jnp.where(causal, ex, -jnp.inf))).astype(BF16)
                    yh = jnp.dot(m, jnp.where(keep, xpb, zero), preferred_element_type=F32)
                    y = yh if y is None else y + yh
                ce = jnp.concatenate([(cg * ecol_s[:, ha * LANES:(ha + 1) * LANES]).astype(BF16),
                                      (cg * ecol_s[:, hb * LANES:(hb + 1) * LANES]).astype(BF16)], axis=1)
                ht = ht_ref[pair]
                y = y + jnp.dot(ce, ht.astype(BF16), preferred_element_type=F32)

                xw = (xp * wexp[:, ps]).astype(BF16)
                dca = ecol_s[c_len - 1:c_len, ha * LANES:(ha + 1) * LANES]
                dcb = ecol_s[c_len - 1:c_len, hb * LANES:(hb + 1) * LANES]
                ht_ref[pair, 0:SSM_N, :] = ht[0:SSM_N] * dca + jnp.dot(
                    bgt, jnp.where(first_head, xw, zero), preferred_element_type=F32)
                ht_ref[pair, SSM_N:2 * SSM_N, :] = ht[SSM_N:] * dcb + jnp.dot(
                    bgt, jnp.where(first_head, zero, xw), preferred_element_type=F32)

                y = y + dexp_ref[0:1, ps] * xp
                y_s[:, ps] = y * _silu(z_ref[rows, ps].astype(F32))

            gs = slice(g * 2 * pair_w, (g + 1) * 2 * pair_w)
            yg = y_s[:, gs]
            yn = yg * lax.rsqrt(jnp.mean(yg * yg, axis=-1, keepdims=True) + EPS)
            o_ref[rows, gs] = (yn * nw_ref[0:1, gs]).astype(BF16)

    for c in range(n_chunks):
        chunk(c, acol_all.at[c], ecol_all.at[c], ats_all.at[c], y_all.at[c])


def _ssd_call(proj, dt_raw, conv_w, conv_b, dt_bias, a_log, d_exp, norm_w, tri3, e_col, e_64, shift3,
              w_down, w_out, tile):
    s = proj.shape[0]
    n_steps = s // tile
    dn_col_blocks = 2 if n_steps % 2 == 0 else 1
    dn_rows = w_down.shape[0] // (n_steps // dn_col_blocks)
    dn_cols = w_down.shape[1] // dn_col_blocks
    out_rows = w_out.shape[0] // n_steps
    assert dn_rows % BF16_ROWS == 0 and out_rows % BF16_ROWS == 0

    def col(k):
        return pl.BlockSpec((tile, D_SSM), lambda i: (i, k))

    def full(shape):
        return pl.BlockSpec(shape, lambda i: (0, 0))

    dn_spec = pl.BlockSpec((dn_rows, dn_cols), lambda i: (i // dn_col_blocks, i % dn_col_blocks))
    out_spec = pl.BlockSpec((out_rows, w_out.shape[1]), lambda i: (i, 0))
    n_pairs = SSM_HEADS // 2
    return pl.pallas_call(
        functools.partial(_ssd_kernel, tile=tile),
        grid=(n_steps,),
        in_specs=[col(4), col(5), col(6),
                  pl.BlockSpec((tile, LANES), lambda i: (i, 0)),
                  pl.BlockSpec((SSM_CONV, D_SSM), lambda i: (0, 0)),
                  pl.BlockSpec((SSM_CONV, D_SSM), lambda i: (0, 1)),
                  pl.BlockSpec((1, D_SSM), lambda i: (0, 0)),
                  pl.BlockSpec((1, D_SSM), lambda i: (0, 1)),
                  full((1, LANES)), full((1, LANES)), full((1, D_SSM)), full((1, D_SSM)),
                  full((CHUNK, 3 * CHUNK)), full(e_col.shape), full(e_64.shape), full(shift3.shape),
                  dn_spec, out_spec],
        out_specs=[pl.BlockSpec((tile, D_SSM), lambda i: (i, 0)), dn_spec, out_spec],
        out_shape=[jax.ShapeDtypeStruct((s, D_SSM), BF16),
                   jax.ShapeDtypeStruct(w_down.shape, BF16),
                   jax.ShapeDtypeStruct(w_out.shape, BF16)],
        scratch_shapes=[pltpu.VMEM((n_pairs, 2 * SSM_N, LANES), F32),
                        pltpu.VMEM((CHUNK, D_SSM), BF16), pltpu.VMEM((CHUNK, D_SSM), BF16),
                        pltpu.VMEM((tile, D_SSM), F32), pltpu.VMEM((tile, D_SSM), F32),
                        pltpu.VMEM((tile // CHUNK, CHUNK, SSM_HEADS * LANES), F32),
                        pltpu.VMEM((tile // CHUNK, CHUNK, SSM_HEADS * LANES), F32),
                        pltpu.VMEM((tile // CHUNK, CHUNK, LANES), F32),
                        pltpu.VMEM((tile // CHUNK, CHUNK, D_SSM), F32)],
        compiler_params=_cparams(1),
        name="ssd",
    )(proj, proj, proj, dt_raw, conv_w, conv_w, conv_b, conv_b, dt_bias, a_log, d_exp, norm_w, tri3,
      e_col, e_64, shift3, w_down, w_out)


def _outproj_kernel(x_ref, oa_ref, ob_ref, wa_ref, wb_ref, g_ref, o_ref):
    mixed = jnp.dot(oa_ref[...], wa_ref[...], preferred_element_type=F32)
    mixed = mixed + jnp.dot(ob_ref[...], wb_ref[...], preferred_element_type=F32)
    o_ref[...] = x_ref[...] + g_ref[...] * mixed


def _outproj_call(x2, o_a, o_b, w_out_bf, mod, tm):
    s = x2.shape[0]
    return pl.pallas_call(
        _outproj_kernel,
        grid=(s // tm,),
        in_specs=[pl.BlockSpec((tm, D_MODEL), lambda i: (i, 0)),
                  pl.BlockSpec((tm, D_HGRN), lambda i: (i, 0)),
                  pl.BlockSpec((tm, D_SSM), lambda i: (i, 0)),
                  pl.BlockSpec((D_HGRN, D_MODEL), lambda i: (0, 0)),
                  pl.BlockSpec((D_SSM, D_MODEL), lambda i: (1, 0)),
                  pl.BlockSpec((1, D_MODEL), lambda i: (0, 2))],
        out_specs=pl.BlockSpec((tm, D_MODEL), lambda i: (i, 0)),
        out_shape=jax.ShapeDtypeStruct((s, D_MODEL), F32),
        compiler_params=_cparams(1),
        name="outproj",
    )(x2, o_a, o_b, w_out_bf, w_out_bf, mod)


def _ffn_kernel(x_ref, nw_ref, sc_ref, sh_ref, gate_ref, fnw_ref, wug_ref, wuv_ref, cwg_ref, cwv_ref,
                cbg_ref, cbv_ref, wd_ref, o_ref, h_ref, ug_ref, uv_ref, act0_ref, act1_ref, carry_ref,
                *, tm, n_slabs):
    i = pl.program_id(0)
    j = pl.program_id(1)
    rb = 128
    act_ref = (act0_ref, act1_ref)

    @pl.when(j == 0)
    def _():
        _norm_mod_rows(x_ref, nw_ref, sc_ref, sh_ref, h_ref, tm)
        o_ref[...] = jnp.zeros_like(o_ref)

    @pl.when((i == 0) & (j < n_slabs))
    def _():
        carry_ref[j] = jnp.zeros(carry_ref.shape[1:], F32)

    def up_conv(slot):
        h = h_ref[...]
        for u_ref, w_ref, which in ((ug_ref, wug_ref, 0), (uv_ref, wuv_ref, 1)):
            u_ref[0:HALO, :] = carry_ref[j, which]
            u_ref[HALO:HALO + tm, :] = jnp.dot(h, w_ref[...], preferred_element_type=F32)
            carry_ref[j, which] = u_ref[tm:tm + HALO, :]

        def conv(u_ref, cw_ref, cb_ref, r):
            acc = cb_ref[...] + cw_ref[FFN_CONV - 1:FFN_CONV, :] * u_ref[HALO + r * rb:HALO + (r + 1) * rb, :]
            for t in range(FFN_CONV - 1):
                off = HALO + r * rb - (FFN_CONV - 1) + t
                acc = acc + cw_ref[t:t + 1, :] * u_ref[off:off + rb, :]
            return acc

        for r in range(tm // rb):
            act_ref[slot][r * rb:(r + 1) * rb, :] = (
                _silu(conv(ug_ref, cwg_ref, cbg_ref, r)) * conv(uv_ref, cwv_ref, cbv_ref, r)).astype(BF16)

    def down(slot):
        o_ref[...] += jnp.dot(act_ref[slot][...], wd_ref[...], preferred_element_type=F32)

    @pl.when(j == 0)
    def _():
        up_conv(0)

    for parity in range(2):
        @pl.when((j > 0) & (j < n_slabs) & (j % 2 == parity))
        def _():
            down(1 - parity)
            up_conv(parity)

    @pl.when(j == n_slabs)
    def _():
        down((n_slabs - 1) % 2)
        gate = gate_ref[...]
        fnw = fnw_ref[...]

        def body(r, carry):
            rows = pl.ds(pl.multiple_of(r * rb, rb), rb)
            y = x_ref[rows, :] + gate * o_ref[rows, :]
            o_ref[rows, :] = y * lax.rsqrt(jnp.mean(y * y, axis=-1, keepdims=True) + EPS) * fnw
            return carry

        lax.fori_loop(0, tm // rb, body, 0)


def _ffn_call(x1, norm_w, mod, final_w, w_up_bf, conv_w, conv_b, w_down_bf, tm, tf):
    s = x1.shape[0]
    n_slabs = D_FF // tf

    def vec(k):
        return pl.BlockSpec((1, D_MODEL), lambda i, j: (0, k))

    def up(j):
        return jnp.minimum(j, n_slabs - 1)

    def dn(j):
        return jnp.maximum(j - 1, 0)

    return pl.pallas_call(
        functools.partial(_ffn_kernel, tm=tm, n_slabs=n_slabs),
        grid=(s // tm, n_slabs + 1),
        in_specs=[pl.BlockSpec((tm, D_MODEL), lambda i, j: (i, 0)),
                  pl.BlockSpec((1, D_MODEL), lambda i, j: (0, 0)),
                  vec(4), vec(3), vec(5),
                  pl.BlockSpec((1, D_MODEL), lambda i, j: (0, 0)),
                  pl.BlockSpec((D_MODEL, tf), lambda i, j: (0, up(j))),
                  pl.BlockSpec((D_MODEL, tf), lambda i, j: (0, n_slabs + up(j))),
                  pl.BlockSpec((FFN_CONV, tf), lambda i, j: (0, up(j))),
                  pl.BlockSpec((FFN_CONV, tf), lambda i, j: (0, n_slabs + up(j))),
                  pl.BlockSpec((1, tf), lambda i, j: (0, up(j))),
                  pl.BlockSpec((1, tf), lambda i, j: (0, n_slabs + up(j))),
                  pl.BlockSpec((tf, D_MODEL), lambda i, j: (dn(j), 0))],
        out_specs=pl.BlockSpec((tm, D_MODEL), lambda i, j: (i, 0)),
        out_shape=jax.ShapeDtypeStruct((s, D_MODEL), F32),
        scratch_shapes=[pltpu.VMEM((tm, D_MODEL), BF16),
                        pltpu.VMEM((tm + HALO, tf), F32), pltpu.VMEM((tm + HALO, tf), F32),
                        pltpu.VMEM((tm, tf), BF16), pltpu.VMEM((tm, tf), BF16),
                        pltpu.VMEM((n_slabs, 2, HALO, tf), F32)],
        compiler_params=_cparams(2),
        name="ffn",
    )(x1, norm_w, mod, mod, mod, final_w, w_up_bf, w_up_bf, conv_w, conv_w, conv_b, conv_b, w_down_bf)


def _tri3(n):
    tri = jnp.tril(jnp.ones((n, n), F32)).astype(BF16)
    return jnp.concatenate([tri, tri, tri], axis=1)


def _expand_matrix(n_terms, width):
    src = jnp.arange(n_terms * LANES) % LANES
    dst = jnp.arange(SSM_HEADS * width) // width
    return (src[:, None] == dst[None, :]).astype(BF16)


def _shift_matrix(n):
    t = jnp.arange((SSM_CONV - 1) * n)
    src = n + t % n - (SSM_CONV - 1) + t // n
    return (src[:, None] == jnp.arange(2 * n)[None, :]).astype(BF16)


def _pad_lanes(v):
    return jnp.pad(v.reshape(1, -1), ((0, 0), (0, LANES - v.shape[-1])))


def kernel(x, c, w_mod, b_mod, norm1_w, w_in, hgrn_lb, hgrn_gnorm_w, ssd_conv_w, ssd_conv_b, ssd_dt_bias,
           ssd_a_log, ssd_d, ssd_norm_w, w_out, norm2_w, ffn_w_up, ffn_conv_w, ffn_conv_b, ffn_w_down,
           final_norm_w):
    bsz, seq, _ = x.shape
    assert bsz == 1 and w_in.shape[0] == 1, "single batch element, single layer"
    assert seq % CHUNK == 0
    x2 = x.reshape(seq, D_MODEL)
    tm_in = min(seq, 1024)
    tile_mix = min(seq, 512)
    tile_ssd = min(seq, 512)
    tm_out = min(seq, 512)
    tm_ffn = min(seq, 1024)

    mod = _mod_call(c.reshape(D_MODEL, 1), w_mod[0], b_mod)

    w_in_bf = w_in[0].astype(BF16)
    w_dt = jnp.pad(w_in_bf[:, D_MAIN:], ((0, 0), (0, LANES - SSM_HEADS)))
    proj, dt_raw = _inproj_call(x2, norm1_w, mod, w_in_bf, w_dt, tm_in, D_MAIN // 4)

    tri3 = _tri3(CHUNK)
    o_a, w_up_bf = _hgrn_call(proj, hgrn_lb, hgrn_gnorm_w, tri3, ffn_w_up[0], tile_mix)
    o_b, w_down_bf, w_out_bf = _ssd_call(
        proj, dt_raw, ssd_conv_w[0], ssd_conv_b, _pad_lanes(ssd_dt_bias[0]), _pad_lanes(ssd_a_log[0]),
        jnp.repeat(ssd_d[0], SSM_P).reshape(1, D_SSM), ssd_norm_w, tri3, _expand_matrix(3, LANES),
        _expand_matrix(2, SSM_P), _shift_matrix(CHUNK), ffn_w_down[0], w_out[0], tile_ssd)

    x1 = _outproj_call(x2, o_a, o_b, w_out_bf, mod, tm_out)

    out = _ffn_call(x1, norm2_w, mod, final_norm_w.reshape(1, D_MODEL), w_up_bf, ffn_conv_w[0], ffn_conv_b,
                    w_down_bf, tm_ffn, 512)
    return out.reshape(bsz, seq, D_MODEL)
```

```python
import functools
import math

import jax
import jax.numpy as jnp
from jax import lax
from jax.experimental import pallas as pl
from jax.experimental.pallas import tpu as pltpu

F32 = jnp.float32
BF16 = jnp.bfloat16

D_MODEL = 2048
D_HGRN = 1024
HGRN_HEADS = 8
HEAD_K = 128
D_SSM = 1024
SSM_HEADS = 16
SSM_P = 64
SSM_GROUPS = 4
SSM_N = 128
SSM_CONV = 4
D_MAIN = 4 * D_HGRN + D_SSM + (D_SSM + 2 * SSM_GROUPS * SSM_N)
D_FF = 5632
FFN_CONV = 3
EPS = 1e-6
LOG2E = math.log2(math.e)

LANES = 128
SUBLANES = 8
BF16_ROWS = 16
CHUNK = 128
CUMSUM_TERMS = 2
FAST_LEVELS = 6
FAST_MAX_LOG2 = 100.0
HALO = SUBLANES
VMEM_LIMIT = 60 * 1024 * 1024

NT_DIMS = (((1,), (1,)), ((), ()))


def _cparams(n_axes):
    return pltpu.CompilerParams(dimension_semantics=("arbitrary",) * n_axes, vmem_limit_bytes=VMEM_LIMIT)


def _sigmoid(x):
    return 1.0 / (1.0 + jnp.exp(-x))


def _silu(x):
    return x * _sigmoid(x)


def _split_bf16(x, n_terms):
    terms = []
    r = x
    for t in range(n_terms):
        p = r.astype(BF16)
        terms.append(p)
        if t + 1 < n_terms:
            r = r - p.astype(F32)
    return terms


def _cumsum_rows(tri3, x):
    c_len = x.shape[0]
    return jnp.dot(tri3[:, :CUMSUM_TERMS * c_len], jnp.concatenate(_split_bf16(x, CUMSUM_TERMS), axis=0),
                   preferred_element_type=F32)


def _mod_kernel(c_ref, w_ref, b_ref, o_ref):
    s = _silu(c_ref[...])
    o_ref[...] = jnp.sum(w_ref[...] * s, axis=0, keepdims=True) + b_ref[...]


def _mod_call(c_col, w_mod, b_mod):
    d, n = w_mod.shape
    tn = 1024
    return pl.pallas_call(
        _mod_kernel,
        grid=(n // tn,),
        in_specs=[pl.BlockSpec((d, 1), lambda j: (0, 0)),
                  pl.BlockSpec((d, tn), lambda j: (0, j)),
                  pl.BlockSpec((1, tn), lambda j: (0, j))],
        out_specs=pl.BlockSpec((1, tn), lambda j: (0, j)),
        out_shape=jax.ShapeDtypeStruct((1, n), F32),
        compiler_params=_cparams(1),
        name="mod",
    )(c_col, w_mod, b_mod)


def _norm_mod_rows(x_ref, nw_ref, sc_ref, sh_ref, h_ref, n_rows, row_block=128):
    w = nw_ref[...] * (1.0 + sc_ref[...])
    sh = sh_ref[...]

    def body(r, carry):
        rows = pl.ds(pl.multiple_of(r * row_block, row_block), row_block)
        x = x_ref[rows, :]
        y = x * lax.rsqrt(jnp.mean(x * x, axis=-1, keepdims=True) + EPS)
        h_ref[rows, :] = (y * w + sh).astype(BF16)
        return carry

    lax.fori_loop(0, n_rows // row_block, body, 0)


def _inproj_kernel(x_ref, nw_ref, sc_ref, sh_ref, w_ref, wdt_ref, o_ref, dt_ref, h_ref, *, tm):
    @pl.when(pl.program_id(1) == 0)
    def _():
        _norm_mod_rows(x_ref, nw_ref, sc_ref, sh_ref, h_ref, tm)
        dt_ref[...] = jnp.dot(h_ref[...], wdt_ref[...], preferred_element_type=F32)

    o_ref[...] = jnp.dot(h_ref[...], w_ref[...], preferred_element_type=F32).astype(BF16)


def _inproj_call(x2, norm_w, mod, w_bf, w_dt, tm, tn):
    s = x2.shape[0]
    return pl.pallas_call(
        functools.partial(_inproj_kernel, tm=tm),
        grid=(s // tm, D_MAIN // tn),
        in_specs=[pl.BlockSpec((tm, D_MODEL), lambda i, j: (i, 0)),
                  pl.BlockSpec((1, D_MODEL), lambda i, j: (0, 0)),
                  pl.BlockSpec((1, D_MODEL), lambda i, j: (0, 1)),
                  pl.BlockSpec((1, D_MODEL), lambda i, j: (0, 0)),
                  pl.BlockSpec((D_MODEL, tn), lambda i, j: (0, j)),
                  pl.BlockSpec((D_MODEL, LANES), lambda i, j: (0, 0))],
        out_specs=[pl.BlockSpec((tm, tn), lambda i, j: (i, j)),
                   pl.BlockSpec((tm, LANES), lambda i, j: (i, 0))],
        out_shape=[jax.ShapeDtypeStruct((s, D_MAIN), BF16),
                   jax.ShapeDtypeStruct((s, LANES), F32)],
        scratch_shapes=[pltpu.VMEM((tm, D_MODEL), BF16)],
        compiler_params=_cparams(2),
        name="inproj",
    )(x2, norm_w, mod, mod, w_bf, w_dt)


def _row_bcast(b_ref, bh, row, n_rows):
    return jnp.broadcast_to(b_ref[bh, row:row + 1, :], (n_rows, LANES))


def _hgrn_kernel(q_ref, f_ref, v_ref, g_ref, lb_ref, gnw_ref, tri_ref, wup_ref, o_ref, wup_bf_ref,
                 st_ref, b_ref, fg_ref, *, n_chunks):
    c_len = CHUNK
    n_levels = c_len.bit_length() - 1

    @pl.when(pl.program_id(0) == 0)
    def _():
        st_ref[...] = jnp.zeros_like(st_ref)

    wup_bf_ref[...] = wup_ref[...].astype(BF16)

    row2 = lax.broadcasted_iota(jnp.int32, (c_len, c_len), 0)
    col2 = lax.broadcasted_iota(jnp.int32, (c_len, c_len), 1)
    xor2 = row2 ^ col2
    level = jnp.where(row2 == col2, n_levels, -1)
    for lvl in range(n_levels):
        level = jnp.where(((xor2 >> lvl) == 1) & (row2 > col2), lvl, level)
    in_block = (level == n_levels) | ((level >= 0) & (level < FAST_LEVELS))
    rowk = lax.broadcasted_iota(jnp.int32, (c_len, LANES), 0)
    sub8 = lax.broadcasted_iota(jnp.int32, (SUBLANES, LANES), 0)
    tri3 = tri_ref[...]

    def block_anchor(bh, blk):
        size = 1 << FAST_LEVELS
        return _row_bcast(b_ref, bh, blk * size + size // 2 - 1, size)

    def gates(c):
        rows = slice(c * c_len, (c + 1) * c_len)
        worst = jnp.zeros((1 << FAST_LEVELS, LANES), F32)
        for h in range(HGRN_HEADS):
            cs = slice(h * HEAD_K, (h + 1) * HEAD_K)
            bh = c * HGRN_HEADS + h
            a0 = lb_ref[0:1, cs]
            a1 = lb_ref[1:2, cs]
            amax = jnp.maximum(a0, a1)
            e0 = jnp.exp(a0 - amax)
            lb = e0 / (e0 + jnp.exp(a1 - amax))
            fg = lb + (1.0 - lb) * _sigmoid(f_ref[rows, cs].astype(F32))
            fg_ref[bh] = fg
            b = _cumsum_rows(tri3, jnp.log(fg) * LOG2E)
            b_ref[bh] = b
            for blk in range(c_len >> FAST_LEVELS):
                span = slice(blk << FAST_LEVELS, (blk + 1) << FAST_LEVELS)
                worst = jnp.maximum(worst, jnp.abs(b[span] - block_anchor(bh, blk)))
        return jnp.max(worst) <= FAST_MAX_LOG2

    def heads(c, fast):
        rows = slice(c * c_len, (c + 1) * c_len)
        for h in range(HGRN_HEADS):
            cs = slice(h * HEAD_K, (h + 1) * HEAD_K)
            bh = c * HGRN_HEADS + h
            qs = _silu(q_ref[rows, cs].astype(F32))
            vb = v_ref[rows, cs]
            fg = fg_ref[bh]
            k = 1.0 - fg
            b = b_ref[bh]

            if fast:
                d = b - jnp.concatenate([block_anchor(bh, blk) for blk in range(c_len >> FAST_LEVELS)], axis=0)
                att = jnp.where(in_block,
                                lax.dot_general((qs * jnp.exp2(d)).astype(BF16), (k * jnp.exp2(-d)).astype(BF16),
                                                NT_DIMS, preferred_element_type=F32), 0.0)
            else:
                att = jnp.where(level == n_levels,
                                lax.dot_general(qs.astype(BF16), k.astype(BF16), NT_DIMS,
                                                preferred_element_type=F32), 0.0)
            for lvl in range(FAST_LEVELS if fast else 0, n_levels):
                m = 1 << lvl
                if m >= SUBLANES:
                    parts = []
                    for blk in range(c_len // (2 * m)):
                        lo = slice(blk * 2 * m, blk * 2 * m + m)
                        up = slice(blk * 2 * m + m, (blk + 1) * 2 * m)
                        bc = _row_bcast(b_ref, bh, blk * 2 * m + m - 1, m)
                        parts.append(k[lo] * jnp.exp2(bc - b[lo]))
                        parts.append(qs[up] * jnp.exp2(b[up] - bc))
                    xl = jnp.concatenate(parts, axis=0)
                else:
                    upper = ((rowk >> lvl) & 1) == 1
                    if lvl == 0:
                        xl = jnp.where(upper, qs * fg, k)
                    else:
                        tiles = []
                        for t8 in range(c_len // SUBLANES):
                            if lvl == 2:
                                tiles.append(_row_bcast(b_ref, bh, t8 * 8 + 3, SUBLANES))
                            else:
                                tiles.append(jnp.where(sub8 < 4, _row_bcast(b_ref, bh, t8 * 8 + 1, SUBLANES),
                                                       _row_bcast(b_ref, bh, t8 * 8 + 5, SUBLANES)))
                        d = b - jnp.concatenate(tiles, axis=0)
                        xl = jnp.where(upper, qs, k) * jnp.exp2(jnp.where(upper, d, -d))
                xb = xl.astype(BF16)
                att = jnp.where(level == lvl,
                                lax.dot_general(xb, xb, NT_DIMS, preferred_element_type=F32), att)

            st = st_ref[h]
            qc = (qs * jnp.exp2(b)).astype(BF16)
            o = jnp.dot(att.astype(BF16), vb, preferred_element_type=F32)
            o = o + lax.dot_general(qc, st.astype(BF16), NT_DIMS, preferred_element_type=F32)
            b_last = b_ref[bh, c_len - 1:c_len, :]
            ke = (k * jnp.exp2(b_last - b)).astype(BF16)
            vt = vb.T
            st_ref[h] = st * jnp.exp2(b_last) + jnp.dot(vt, ke, preferred_element_type=F32)

            on = o * lax.rsqrt(jnp.mean(o * o, axis=-1, keepdims=True) + EPS)
            o_ref[rows, cs] = (on * gnw_ref[0:1, cs] * _silu(g_ref[rows, cs].astype(F32))).astype(BF16)

    fast_ok = gates(0)
    for c in range(1, n_chunks):
        fast_ok = fast_ok & gates(c)

    @pl.when(fast_ok)
    def _():
        for c in range(n_chunks):
            heads(c, True)

    @pl.when(jnp.logical_not(fast_ok))
    def _():
        for c in range(n_chunks):
            heads(c, False)


def _hgrn_call(proj, hgrn_lb, gnorm_w, tri3, w_up, tile):
    s = proj.shape[0]
    n_steps = s // tile
    up_rows, up_cols = w_up.shape
    assert up_rows % (n_steps * BF16_ROWS) == 0

    def col(k):
        return pl.BlockSpec((tile, D_HGRN), lambda i: (i, k))

    return pl.pallas_call(
        functools.partial(_hgrn_kernel, n_chunks=tile // CHUNK),
        grid=(n_steps,),
        in_specs=[col(0), col(1), col(2), col(3),
                  pl.BlockSpec((2, D_HGRN), lambda i: (0, 0)),
                  pl.BlockSpec((1, D_HGRN), lambda i: (0, 0)),
                  pl.BlockSpec((CHUNK, 3 * CHUNK), lambda i: (0, 0)),
                  pl.BlockSpec((up_rows // n_steps, up_cols), lambda i: (i, 0))],
        out_specs=[pl.BlockSpec((tile, D_HGRN), lambda i: (i, 0)),
                   pl.BlockSpec((up_rows // n_steps, up_cols), lambda i: (i, 0))],
        out_shape=[jax.ShapeDtypeStruct((s, D_HGRN), BF16),
                   jax.ShapeDtypeStruct(w_up.shape, BF16)],
        scratch_shapes=[pltpu.VMEM((HGRN_HEADS, HEAD_K, HEAD_K), F32),
                        pltpu.VMEM((tile // CHUNK * HGRN_HEADS, CHUNK, LANES), F32),
                        pltpu.VMEM((tile // CHUNK * HGRN_HEADS, CHUNK, LANES), F32)],
        compiler_params=_cparams(1),
        name="hgrn",
    )(proj, proj, proj, proj, hgrn_lb, gnorm_w, tri3, w_up)


def _ssd_kernel(z_ref, xr_ref, bcr_ref, dt_ref, cwx_ref, cwbc_ref, cbx_ref, cbbc_ref, dtb_ref, alog_ref,
                dexp_ref, nw_ref, tri_ref, ecol_ref, e64_ref, shift_ref, wdn_ref, wout_ref,
                o_ref, wdn_bf_ref, wout_bf_ref,
                ht_ref, halo_x, halo_bc, cx_ref, cbc_ref, acol_all, ecol_all, ats_all, y_all, *, tile):
    c_len = CHUNK
    n_chunks = tile // c_len
    pair_w = 2 * SSM_P

    @pl.when(pl.program_id(0) == 0)
    def _():
        ht_ref[...] = jnp.zeros_like(ht_ref)
        halo_x[...] = jnp.zeros_like(halo_x)
        halo_bc[...] = jnp.zeros_like(halo_bc)

    wdn_bf_ref[...] = wdn_ref[...].astype(BF16)
    wout_bf_ref[...] = wout_ref[...].astype(BF16)

    slab = 512
    for raw_ref, halo, cw_ref, cb_ref, out in ((xr_ref, halo_x, cwx_ref, cbx_ref, cx_ref),
                                               (bcr_ref, halo_bc, cwbc_ref, cbbc_ref, cbc_ref)):
        for r in range(n_chunks):
            for s0 in range(0, D_SSM, slab):
                cols = slice(s0, s0 + slab)
                cur = raw_ref[r * c_len:(r + 1) * c_len, cols]
                prev = halo[:, cols] if r == 0 else raw_ref[(r - 1) * c_len:r * c_len, cols]
                sh = jnp.dot(shift_ref[...], jnp.concatenate([prev, cur], axis=0), preferred_element_type=F32)
                acc = cb_ref[:, cols] + cw_ref[SSM_CONV - 1:SSM_CONV, cols] * cur.astype(F32)
                for j in range(SSM_CONV - 1):
                    acc = acc + cw_ref[j:j + 1, cols] * sh[j * c_len:(j + 1) * c_len]
                out[r * c_len:(r + 1) * c_len, cols] = _silu(acc)
        halo[...] = raw_ref[tile - c_len:tile, :]

    lane = lax.broadcasted_iota(jnp.int32, (c_len, LANES), 1)
    row2 = lax.broadcasted_iota(jnp.int32, (c_len, c_len), 0)
    col2 = lax.broadcasted_iota(jnp.int32, (c_len, c_len), 1)
    causal = row2 >= col2
    first_head = lane < SSM_P
    head_lane = lax.broadcasted_iota(jnp.int32, (1, LANES), 1) < SSM_HEADS
    a_head = jnp.where(head_lane, -jnp.exp(alog_ref[...]) * LOG2E, 0.0)
    tri3 = tri_ref[...]

    def chunk(c, acol_s, ecol_s, ats_s, y_s):
        rows = slice(c * c_len, (c + 1) * c_len)
        dt = jax.nn.softplus(dt_ref[rows, :] + dtb_ref[...])
        acum = _cumsum_rows(tri3, dt * a_head)
        a_last = acum[c_len - 1:c_len, :]
        acol_s[...] = jnp.dot(jnp.concatenate(_split_bf16(acum, 3), axis=1), ecol_ref[...],
                              preferred_element_type=F32)
        ecol_s[...] = jnp.dot(jnp.concatenate(_split_bf16(jnp.exp2(acum), 2), axis=1),
                              ecol_ref[0:2 * LANES, :], preferred_element_type=F32)
        wdec = dt * jnp.exp2(a_last - acum)
        wexp = jnp.dot(jnp.concatenate(_split_bf16(wdec, 2), axis=1), e64_ref[...],
                       preferred_element_type=F32)
        ats_s[...] = (acum - jnp.log(dt) * LOG2E).T

        for g in range(SSM_GROUPS):
            bg = cbc_ref[rows, g * SSM_N:(g + 1) * SSM_N]
            cg = cbc_ref[rows, (SSM_GROUPS + g) * SSM_N:(SSM_GROUPS + g + 1) * SSM_N]
            cb = lax.dot_general(cg.astype(BF16), bg.astype(BF16), NT_DIMS, preferred_element_type=F32)
            bgt = bg.T.astype(BF16)
            for pp in range(2):
                pair = 2 * g + pp
                ha, hb = 2 * pair, 2 * pair + 1
                ps = slice(pair * pair_w, (pair + 1) * pair_w)
                xp = cx_ref[rows, ps]
                xpb = xp.astype(BF16)
                zero = jnp.zeros_like(xpb)
                y = None
                for hh, keep in ((ha, first_head), (hb, ~first_head)):
                    ex = acol_s[:, hh * LANES:(hh + 1) * LANES] - ats_s[hh:hh + 1, :]
                    m = (cb * jnp.exp2(jnp.where(causal, ex, -jnp.inf))).astype(BF16)
                    yh = jnp.dot(m, jnp.where(keep, xpb, zero), preferred_element_type=F32)
                    y = yh if y is None else y + yh
                ce = jnp.concatenate([(cg * ecol_s[:, ha * LANES:(ha + 1) * LANES]).astype(BF16),
                                      (cg * ecol_s[:, hb * LANES:(hb + 1) * LANES]).astype(BF16)], axis=1)
                ht = ht_ref[pair]
                y = y + jnp.dot(ce, ht.astype(BF16), preferred_element_type=F32)

                xw = (xp * wexp[:, ps]).astype(BF16)
                dca = ecol_s[c_len - 1:c_len, ha * LANES:(ha + 1) * LANES]
                dcb = ecol_s[c_len - 1:c_len, hb * LANES:(hb + 1) * LANES]
                ht_ref[pair, 0:SSM_N, :] = ht[0:SSM_N] * dca + jnp.dot(
                    bgt, jnp.where(first_head, xw, zero), preferred_element_type=F32)
                ht_ref[pair, SSM_N:2 * SSM_N, :] = ht[SSM_N:] * dcb + jnp.dot(
                    bgt, jnp.where(first_head, zero, xw), preferred_element_type=F32)

                y = y + dexp_ref[0:1, ps] * xp
                y_s[:, ps] = y * _silu(z_ref[rows, ps].astype(F32))

            gs = slice(g * 2 * pair_w, (g + 1) * 2 * pair_w)
            yg = y_s[:, gs]
            yn = yg * lax.rsqrt(jnp.mean(yg * yg, axis=-1, keepdims=True) + EPS)
            o_ref[rows, gs] = (yn * nw_ref[0:1, gs]).astype(BF16)

    for c in range(n_chunks):
        chunk(c, acol_all.at[c], ecol_all.at[c], ats_all.at[c], y_all.at[c])


def _ssd_call(proj, dt_raw, conv_w, conv_b, dt_bias, a_log, d_exp, norm_w, tri3, e_col, e_64, shift3,
              w_down, w_out, tile):
    s = proj.shape[0]
    n_steps = s // tile
    dn_col_blocks = 2 if n_steps % 2 == 0 else 1
    dn_rows = w_down.shape[0] // (n_steps // dn_col_blocks)
    dn_cols = w_down.shape[1] // dn_col_blocks
    out_rows = w_out.shape[0] // n_steps
    assert dn_rows % BF16_ROWS == 0 and out_rows % BF16_ROWS == 0

    def col(k):
        return pl.BlockSpec((tile, D_SSM), lambda i: (i, k))

    def full(shape):
        return pl.BlockSpec(shape, lambda i: (0, 0))

    dn_spec = pl.BlockSpec((dn_rows, dn_cols), lambda i: (i // dn_col_blocks, i % dn_col_blocks))
    out_spec = pl.BlockSpec((out_rows, w_out.shape[1]), lambda i: (i, 0))
    n_pairs = SSM_HEADS // 2
    return pl.pallas_call(
        functools.partial(_ssd_kernel, tile=tile),
        grid=(n_steps,),
        in_specs=[col(4), col(5), col(6),
                  pl.BlockSpec((tile, LANES), lambda i: (i, 0)),
                  pl.BlockSpec((SSM_CONV, D_SSM), lambda i: (0, 0)),
                  pl.BlockSpec((SSM_CONV, D_SSM), lambda i: (0, 1)),
                  pl.BlockSpec((1, D_SSM), lambda i: (0, 0)),
                  pl.BlockSpec((1, D_SSM), lambda i: (0, 1)),
                  full((1, LANES)), full((1, LANES)), full((1, D_SSM)), full((1, D_SSM)),
                  full((CHUNK, 3 * CHUNK)), full(e_col.shape), full(e_64.shape), full(shift3.shape),
                  dn_spec, out_spec],
        out_specs=[pl.BlockSpec((tile, D_SSM), lambda i: (i, 0)), dn_spec, out_spec],
        out_shape=[jax.ShapeDtypeStruct((s, D_SSM), BF16),
                   jax.ShapeDtypeStruct(w_down.shape, BF16),
                   jax.ShapeDtypeStruct(w_out.shape, BF16)],
        scratch_shapes=[pltpu.VMEM((n_pairs, 2 * SSM_N, LANES), F32),
                        pltpu.VMEM((CHUNK, D_SSM), BF16), pltpu.VMEM((CHUNK, D_SSM), BF16),
                        pltpu.VMEM((tile, D_SSM), F32), pltpu.VMEM((tile, D_SSM), F32),
                        pltpu.VMEM((tile // CHUNK, CHUNK, SSM_HEADS * LANES), F32),
                        pltpu.VMEM((tile // CHUNK, CHUNK, SSM_HEADS * LANES), F32),
                        pltpu.VMEM((tile // CHUNK, CHUNK, LANES), F32),
                        pltpu.VMEM((tile // CHUNK, CHUNK, D_SSM), F32)],
        compiler_params=_cparams(1),
        name="ssd",
    )(proj, proj, proj, dt_raw, conv_w, conv_w, conv_b, conv_b, dt_bias, a_log, d_exp, norm_w, tri3,
      e_col, e_64, shift3, w_down, w_out)


def _outproj_kernel(x_ref, oa_ref, ob_ref, wa_ref, wb_ref, g_ref, o_ref):
    mixed = jnp.dot(oa_ref[...], wa_ref[...], preferred_element_type=F32)
    mixed = mixed + jnp.dot(ob_ref[...], wb_ref[...], preferred_element_type=F32)
    o_ref[...] = x_ref[...] + g_ref[...] * mixed


def _outproj_call(x2, o_a, o_b, w_out_bf, mod, tm):
    s = x2.shape[0]
    return pl.pallas_call(
        _outproj_kernel,
        grid=(s // tm,),
        in_specs=[pl.BlockSpec((tm, D_MODEL), lambda i: (i, 0)),
                  pl.BlockSpec((tm, D_HGRN), lambda i: (i, 0)),
                  pl.BlockSpec((tm, D_SSM), lambda i: (i, 0)),
                  pl.BlockSpec((D_HGRN, D_MODEL), lambda i: (0, 0)),
                  pl.BlockSpec((D_SSM, D_MODEL), lambda i: (1, 0)),
                  pl.BlockSpec((1, D_MODEL), lambda i: (0, 2))],
        out_specs=pl.BlockSpec((tm, D_MODEL), lambda i: (i, 0)),
        out_shape=jax.ShapeDtypeStruct((s, D_MODEL), F32),
        compiler_params=_cparams(1),
        name="outproj",
    )(x2, o_a, o_b, w_out_bf, w_out_bf, mod)


def _ffn_kernel(x_ref, nw_ref, sc_ref, sh_ref, gate_ref, fnw_ref, wug_ref, wuv_ref, cwg_ref, cwv_ref,
                cbg_ref, cbv_ref, wd_ref, o_ref, h_ref, ug_ref, uv_ref, act0_ref, act1_ref, carry_ref,
                *, tm, n_slabs):
    i = pl.program_id(0)
    j = pl.program_id(1)
    rb = 128
    act_ref = (act0_ref, act1_ref)

    @pl.when(j == 0)
    def _():
        _norm_mod_rows(x_ref, nw_ref, sc_ref, sh_ref, h_ref, tm)
        o_ref[...] = jnp.zeros_like(o_ref)

    @pl.when((i == 0) & (j < n_slabs))
    def _():
        carry_ref[j] = jnp.zeros(carry_ref.shape[1:], F32)

    def up_conv(slot):
        h = h_ref[...]
        for u_ref, w_ref, which in ((ug_ref, wug_ref, 0), (uv_ref, wuv_ref, 1)):
            u_ref[0:HALO, :] = carry_ref[j, which]
            u_ref[HALO:HALO + tm, :] = jnp.dot(h, w_ref[...], preferred_element_type=F32)
            carry_ref[j, which] = u_ref[tm:tm + HALO, :]

        def conv(u_ref, cw_ref, cb_ref, r):
            acc = cb_ref[...] + cw_ref[FFN_CONV - 1:FFN_CONV, :] * u_ref[HALO + r * rb:HALO + (r + 1) * rb, :]
            for t in range(FFN_CONV - 1):
                off = HALO + r * rb - (FFN_CONV - 1) + t
                acc = acc + cw_ref[t:t + 1, :] * u_ref[off:off + rb, :]
            return acc

        for r in range(tm // rb):
            act_ref[slot][r * rb:(r + 1) * rb, :] = (
                _silu(conv(ug_ref, cwg_ref, cbg_ref, r)) * conv(uv_ref, cwv_ref, cbv_ref, r)).astype(BF16)

    def down(slot):
        o_ref[...] += jnp.dot(act_ref[slot][...], wd_ref[...], preferred_element_type=F32)

    @pl.when(j == 0)
    def _():
        up_conv(0)

    for parity in range(2):
        @pl.when((j > 0) & (j < n_slabs) & (j % 2 == parity))
        def _():
            down(1 - parity)
            up_conv(parity)

    @pl.when(j == n_slabs)
    def _():
        down((n_slabs - 1) % 2)
        gate = gate_ref[...]
        fnw = fnw_ref[...]

        def body(r, carry):
            rows = pl.ds(pl.multiple_of(r * rb, rb), rb)
            y = x_ref[rows, :] + gate * o_ref[rows, :]
            o_ref[rows, :] = y * lax.rsqrt(jnp.mean(y * y, axis=-1, keepdims=True) + EPS) * fnw
            return carry

        lax.fori_loop(0, tm // rb, body, 0)


def _ffn_call(x1, norm_w, mod, final_w, w_up_bf, conv_w, conv_b, w_down_bf, tm, tf):
    s = x1.shape[0]
    n_slabs = D_FF // tf

    def vec(k):
        return pl.BlockSpec((1, D_MODEL), lambda i, j: (0, k))

    def up(j):
        return jnp.minimum(j, n_slabs - 1)

    def dn(j):
        return jnp.maximum(j - 1, 0)

    return pl.pallas_call(
        functools.partial(_ffn_kernel, tm=tm, n_slabs=n_slabs),
        grid=(s // tm, n_slabs + 1),
        in_specs=[pl.BlockSpec((tm, D_MODEL), lambda i, j: (i, 0)),
                  pl.BlockSpec((1, D_MODEL), lambda i, j: (0, 0)),
                  vec(4), vec(3), vec(5),
                  pl.BlockSpec((1, D_MODEL), lambda i, j: (0, 0)),
                  pl.BlockSpec((D_MODEL, tf), lambda i, j: (0, up(j))),
                  pl.BlockSpec((D_MODEL, tf), lambda i, j: (0, n_slabs + up(j))),
                  pl.BlockSpec((FFN_CONV, tf), lambda i, j: (0, up(j))),
                  pl.BlockSpec((FFN_CONV, tf), lambda i, j: (0, n_slabs + up(j))),
                  pl.BlockSpec((1, tf), lambda i, j: (0, up(j))),
                  pl.BlockSpec((1, tf), lambda i, j: (0, n_slabs + up(j))),
                  pl.BlockSpec((tf, D_MODEL), lambda i, j: (dn(j), 0))],
        out_specs=pl.BlockSpec((tm, D_MODEL), lambda i, j: (i, 0)),
        out_shape=jax.ShapeDtypeStruct((s, D_MODEL), F32),
        scratch_shapes=[pltpu.VMEM((tm, D_MODEL), BF16),
                        pltpu.VMEM((tm + HALO, tf), F32), pltpu.VMEM((tm + HALO, tf), F32),
                        pltpu.VMEM((tm, tf), BF16), pltpu.VMEM((tm, tf), BF16),
                        pltpu.VMEM((n_slabs, 2, HALO, tf), F32)],
        compiler_params=_cparams(2),
        name="ffn",
    )(x1, norm_w, mod, mod, mod, final_w, w_up_bf, w_up_bf, conv_w, conv_w, conv_b, conv_b, w_down_bf)


def _tri3(n):
    tri = jnp.tril(jnp.ones((n, n), F32)).astype(BF16)
    return jnp.concatenate([tri, tri, tri], axis=1)


def _expand_matrix(n_terms, width):
    src = jnp.arange(n_terms * LANES) % LANES
    dst = jnp.arange(SSM_HEADS * width) // width
    return (src[:, None] == dst[None, :]).astype(BF16)


def _shift_matrix(n):
    t = jnp.arange((SSM_CONV - 1) * n)
    src = n + t % n - (SSM_CONV - 1) + t // n
    return (src[:, None] == jnp.arange(2 * n)[None, :]).astype(BF16)


def _pad_lanes(v):
    return jnp.pad(v.reshape(1, -1), ((0, 0), (0, LANES - v.shape[-1])))


def kernel(x, c, w_mod, b_mod, norm1_w, w_in, hgrn_lb, hgrn_gnorm_w, ssd_conv_w, ssd_conv_b, ssd_dt_bias,
           ssd_a_log, ssd_d, ssd_norm_w, w_out, norm2_w, ffn_w_up, ffn_conv_w, ffn_conv_b, ffn_w_down,
           final_norm_w):
    bsz, seq, _ = x.shape
    assert bsz == 1 and w_in.shape[0] == 1, "single batch element, single layer"
    assert seq % CHUNK == 0
    x2 = x.reshape(seq, D_MODEL)
    tm_in = min(seq, 1024)
    tile_mix = min(seq, 512)
    tile_ssd = min(seq, 512)
    tm_out = min(seq, 512)
    tm_ffn = min(seq, 1024)

    mod = _mod_call(c.reshape(D_MODEL, 1), w_mod[0], b_mod)

    w_in_bf = w_in[0].astype(BF16)
    w_dt = jnp.pad(w_in_bf[:, D_MAIN:], ((0, 0), (0, LANES - SSM_HEADS)))
    proj, dt_raw = _inproj_call(x2, norm1_w, mod, w_in_bf, w_dt, tm_in, D_MAIN // 4)

    tri3 = _tri3(CHUNK)
    o_a, w_up_bf = _hgrn_call(proj, hgrn_lb, hgrn_gnorm_w, tri3, ffn_w_up[0], tile_mix)
    o_b, w_down_bf, w_out_bf = _ssd_call(
        proj, dt_raw, ssd_conv_w[0], ssd_conv_b, _pad_lanes(ssd_dt_bias[0]), _pad_lanes(ssd_a_log[0]),
        jnp.repeat(ssd_d[0], SSM_P).reshape(1, D_SSM), ssd_norm_w, tri3, _expand_matrix(3, LANES),
        _expand_matrix(2, SSM_P), _shift_matrix(CHUNK), ffn_w_down[0], w_out[0], tile_ssd)

    x1 = _outproj_call(x2, o_a, o_b, w_out_bf, mod, tm_out)

    out = _ffn_call(x1, norm2_w, mod, final_norm_w.reshape(1, D_MODEL), w_up_bf, ffn_conv_w[0], ffn_conv_b,
                    w_down_bf, tm_ffn, 512)
    return out.reshape(bsz, seq, D_MODEL)
```

```python
import functools
import math

import jax
import jax.numpy as jnp
from jax import lax
from jax.experimental import pallas as pl
from jax.experimental.pallas import tpu as pltpu

F32 = jnp.float32
BF16 = jnp.bfloat16

D_MODEL = 2048
D_HGRN = 1024
HGRN_HEADS = 8
HEAD_K = 128
D_SSM = 1024
SSM_HEADS = 16
SSM_P = 64
SSM_GROUPS = 4
SSM_N = 128
SSM_CONV = 4
D_MAIN = 4 * D_HGRN + D_SSM + (D_SSM + 2 * SSM_GROUPS * SSM_N)
D_FF = 5632
FFN_CONV = 3
EPS = 1e-6
LOG2E = math.log2(math.e)

LANES = 128
SUBLANES = 8
BF16_ROWS = 16
CHUNK = 128
CUMSUM_TERMS = 2
FAST_LEVELS = 6
FAST_MAX_LOG2 = 100.0
HALO = SUBLANES
VMEM_LIMIT = 60 * 1024 * 1024

MATMUL_ROWS = 1024
MIXER_ROWS = 512
INPROJ_COL_BLOCKS = 4
FF_SLAB = 512

NT_DIMS = (((1,), (1,)), ((), ()))


def _cparams(n_axes):
    return pltpu.CompilerParams(dimension_semantics=("arbitrary",) * n_axes, vmem_limit_bytes=VMEM_LIMIT)


def _sigmoid(x):
    return 1.0 / (1.0 + jnp.exp(-x))


def _silu(x):
    return x * _sigmoid(x)


def _split_bf16(x, n_terms):
    terms = []
    r = x
    for t in range(n_terms):
        p = r.astype(BF16)
        terms.append(p)
        if t + 1 < n_terms:
            r = r - p.astype(F32)
    return terms


def _cumsum_rows(tri3, x):
    c_len = x.shape[0]
    return jnp.dot(tri3[:, :CUMSUM_TERMS * c_len], jnp.concatenate(_split_bf16(x, CUMSUM_TERMS), axis=0),
                   preferred_element_type=F32)


def _mod_kernel(c_ref, w_ref, b_ref, o_ref):
    s = _silu(c_ref[...])
    o_ref[...] = jnp.sum(w_ref[...] * s, axis=0, keepdims=True) + b_ref[...]


def _mod_call(c_col, w_mod, b_mod):
    d, n = w_mod.shape
    tn = 1024
    return pl.pallas_call(
        _mod_kernel,
        grid=(n // tn,),
        in_specs=[pl.BlockSpec((d, 1), lambda j: (0, 0)),
                  pl.BlockSpec((d, tn), lambda j: (0, j)),
                  pl.BlockSpec((1, tn), lambda j: (0, j))],
        out_specs=pl.BlockSpec((1, tn), lambda j: (0, j)),
        out_shape=jax.ShapeDtypeStruct((1, n), F32),
        compiler_params=_cparams(1),
        name="mod",
    )(c_col, w_mod, b_mod)


def _norm_mod_rows(x_ref, nw_ref, sc_ref, sh_ref, h_ref, n_rows, row_block=128):
    w = nw_ref[...] * (1.0 + sc_ref[...])
    sh = sh_ref[...]

    def body(r, carry):
        rows = pl.ds(pl.multiple_of(r * row_block, row_block), row_block)
        x = x_ref[rows, :]
        y = x * lax.rsqrt(jnp.mean(x * x, axis=-1, keepdims=True) + EPS)
        h_ref[rows, :] = (y * w + sh).astype(BF16)
        return carry

    lax.fori_loop(0, n_rows // row_block, body, 0)


def _inproj_kernel(x_ref, nw_ref, sc_ref, sh_ref, w_ref, wdt_ref, o_ref, dt_ref, h_ref, *, tm):
    @pl.when(pl.program_id(1) == 0)
    def _():
        _norm_mod_rows(x_ref, nw_ref, sc_ref, sh_ref, h_ref, tm)
        dt_ref[...] = jnp.dot(h_ref[...], wdt_ref[...], preferred_element_type=F32)

    o_ref[...] = jnp.dot(h_ref[...], w_ref[...], preferred_element_type=F32).astype(BF16)


def _inproj_call(x2, norm_w, mod, w_bf, w_dt, tm, tn):
    s = x2.shape[0]
    return pl.pallas_call(
        functools.partial(_inproj_kernel, tm=tm),
        grid=(s // tm, D_MAIN // tn),
        in_specs=[pl.BlockSpec((tm, D_MODEL), lambda i, j: (i, 0)),
                  pl.BlockSpec((1, D_MODEL), lambda i, j: (0, 0)),
                  pl.BlockSpec((1, D_MODEL), lambda i, j: (0, 1)),
                  pl.BlockSpec((1, D_MODEL), lambda i, j: (0, 0)),
                  pl.BlockSpec((D_MODEL, tn), lambda i, j: (0, j)),
                  pl.BlockSpec((D_MODEL, LANES), lambda i, j: (0, 0))],
        out_specs=[pl.BlockSpec((tm, tn), lambda i, j: (i, j)),
                   pl.BlockSpec((tm, LANES), lambda i, j: (i, 0))],
        out_shape=[jax.ShapeDtypeStruct((s, D_MAIN), BF16),
                   jax.ShapeDtypeStruct((s, LANES), F32)],
        scratch_shapes=[pltpu.VMEM((tm, D_MODEL), BF16)],
        compiler_params=_cparams(2),
        name="inproj",
    )(x2, norm_w, mod, mod, w_bf, w_dt)


def _row_bcast(b_ref, bh, row, n_rows):
    return jnp.broadcast_to(b_ref[bh, row:row + 1, :], (n_rows, LANES))


def _hgrn_kernel(q_ref, f_ref, v_ref, g_ref, lb_ref, gnw_ref, tri_ref, wup_ref, o_ref, wup_bf_ref,
                 st_ref, b_ref, fg_ref, *, n_chunks):
    c_len = CHUNK
    n_levels = c_len.bit_length() - 1

    @pl.when(pl.program_id(0) == 0)
    def _():
        st_ref[...] = jnp.zeros_like(st_ref)

    wup_bf_ref[...] = wup_ref[...].astype(BF16)

    row2 = lax.broadcasted_iota(jnp.int32, (c_len, c_len), 0)
    col2 = lax.broadcasted_iota(jnp.int32, (c_len, c_len), 1)
    xor2 = row2 ^ col2
    level = jnp.where(row2 == col2, n_levels, -1)
    for lvl in range(n_levels):
        level = jnp.where(((xor2 >> lvl) == 1) & (row2 > col2), lvl, level)
    in_block = (level == n_levels) | ((level >= 0) & (level < FAST_LEVELS))
    rowk = lax.broadcasted_iota(jnp.int32, (c_len, LANES), 0)
    sub8 = lax.broadcasted_iota(jnp.int32, (SUBLANES, LANES), 0)
    tri3 = tri_ref[...]

    def block_anchor(bh, blk):
        size = 1 << FAST_LEVELS
        return _row_bcast(b_ref, bh, blk * size + size // 2 - 1, size)

    def gates(c):
        rows = slice(c * c_len, (c + 1) * c_len)
        worst = jnp.zeros((1 << FAST_LEVELS, LANES), F32)
        for h in range(HGRN_HEADS):
            cs = slice(h * HEAD_K, (h + 1) * HEAD_K)
            bh = c * HGRN_HEADS + h
            a0 = lb_ref[0:1, cs]
            a1 = lb_ref[1:2, cs]
            amax = jnp.maximum(a0, a1)
            e0 = jnp.exp(a0 - amax)
            lb = e0 / (e0 + jnp.exp(a1 - amax))
            fg = lb + (1.0 - lb) * _sigmoid(f_ref[rows, cs].astype(F32))
            fg_ref[bh] = fg
            b = _cumsum_rows(tri3, jnp.log(fg) * LOG2E)
            b_ref[bh] = b
            for blk in range(c_len >> FAST_LEVELS):
                span = slice(blk << FAST_LEVELS, (blk + 1) << FAST_LEVELS)
                worst = jnp.maximum(worst, jnp.abs(b[span] - block_anchor(bh, blk)))
        return jnp.max(worst) <= FAST_MAX_LOG2

    def heads(c, fast):
        rows = slice(c * c_len, (c + 1) * c_len)
        for h in range(HGRN_HEADS):
            cs = slice(h * HEAD_K, (h + 1) * HEAD_K)
            bh = c * HGRN_HEADS + h
            qs = _silu(q_ref[rows, cs].astype(F32))
            vb = v_ref[rows, cs]
            fg = fg_ref[bh]
            k = 1.0 - fg
            b = b_ref[bh]

            if fast:
                d = b - jnp.concatenate([block_anchor(bh, blk) for blk in range(c_len >> FAST_LEVELS)], axis=0)
                att = jnp.where(in_block,
                                lax.dot_general((qs * jnp.exp2(d)).astype(BF16), (k * jnp.exp2(-d)).astype(BF16),
                                                NT_DIMS, preferred_element_type=F32), 0.0)
            else:
                att = jnp.where(level == n_levels,
                                lax.dot_general(qs.astype(BF16), k.astype(BF16), NT_DIMS,
                                                preferred_element_type=F32), 0.0)
            for lvl in range(FAST_LEVELS if fast else 0, n_levels):
                m = 1 << lvl
                if m >= SUBLANES:
                    parts = []
                    for blk in range(c_len // (2 * m)):
                        lo = slice(blk * 2 * m, blk * 2 * m + m)
                        up = slice(blk * 2 * m + m, (blk + 1) * 2 * m)
                        bc = _row_bcast(b_ref, bh, blk * 2 * m + m - 1, m)
                        parts.append(k[lo] * jnp.exp2(bc - b[lo]))
                        parts.append(qs[up] * jnp.exp2(b[up] - bc))
                    xl = jnp.concatenate(parts, axis=0)
                else:
                    upper = ((rowk >> lvl) & 1) == 1
                    if lvl == 0:
                        xl = jnp.where(upper, qs * fg, k)
                    else:
                        tiles = []
                        for t8 in range(c_len // SUBLANES):
                            if lvl == 2:
                                tiles.append(_row_bcast(b_ref, bh, t8 * 8 + 3, SUBLANES))
                            else:
                                tiles.append(jnp.where(sub8 < 4, _row_bcast(b_ref, bh, t8 * 8 + 1, SUBLANES),
                                                       _row_bcast(b_ref, bh, t8 * 8 + 5, SUBLANES)))
                        d = b - jnp.concatenate(tiles, axis=0)
                        xl = jnp.where(upper, qs, k) * jnp.exp2(jnp.where(upper, d, -d))
                xb = xl.astype(BF16)
                att = jnp.where(level == lvl,
                                lax.dot_general(xb, xb, NT_DIMS, preferred_element_type=F32), att)

            st = st_ref[h]
            qc = (qs * jnp.exp2(b)).astype(BF16)
            o = jnp.dot(att.astype(BF16), vb, preferred_element_type=F32)
            o = o + lax.dot_general(qc, st.astype(BF16), NT_DIMS, preferred_element_type=F32)
            b_last = b_ref[bh, c_len - 1:c_len, :]
            ke = (k * jnp.exp2(b_last - b)).astype(BF16)
            vt = vb.T
            st_ref[h] = st * jnp.exp2(b_last) + jnp.dot(vt, ke, preferred_element_type=F32)

            on = o * lax.rsqrt(jnp.mean(o * o, axis=-1, keepdims=True) + EPS)
            o_ref[rows, cs] = (on * gnw_ref[0:1, cs] * _silu(g_ref[rows, cs].astype(F32))).astype(BF16)

    fast_ok = gates(0)
    for c in range(1, n_chunks):
        fast_ok = fast_ok & gates(c)

    @pl.when(fast_ok)
    def _():
        for c in range(n_chunks):
            heads(c, True)

    @pl.when(jnp.logical_not(fast_ok))
    def _():
        for c in range(n_chunks):
            heads(c, False)


def _hgrn_call(proj, hgrn_lb, gnorm_w, tri3, w_up, tile):
    s = proj.shape[0]
    n_steps = s // tile
    up_rows, up_cols = w_up.shape
    assert up_rows % (n_steps * BF16_ROWS) == 0

    def col(k):
        return pl.BlockSpec((tile, D_HGRN), lambda i: (i, k))

    return pl.pallas_call(
        functools.partial(_hgrn_kernel, n_chunks=tile // CHUNK),
        grid=(n_steps,),
        in_specs=[col(0), col(1), col(2), col(3),
                  pl.BlockSpec((2, D_HGRN), lambda i: (0, 0)),
                  pl.BlockSpec((1, D_HGRN), lambda i: (0, 0)),
                  pl.BlockSpec((CHUNK, 3 * CHUNK), lambda i: (0, 0)),
                  pl.BlockSpec((up_rows // n_steps, up_cols), lambda i: (i, 0))],
        out_specs=[pl.BlockSpec((tile, D_HGRN), lambda i: (i, 0)),
                   pl.BlockSpec((up_rows // n_steps, up_cols), lambda i: (i, 0))],
        out_shape=[jax.ShapeDtypeStruct((s, D_HGRN), BF16),
                   jax.ShapeDtypeStruct(w_up.shape, BF16)],
        scratch_shapes=[pltpu.VMEM((HGRN_HEADS, HEAD_K, HEAD_K), F32),
                        pltpu.VMEM((tile // CHUNK * HGRN_HEADS, CHUNK, LANES), F32),
                        pltpu.VMEM((tile // CHUNK * HGRN_HEADS, CHUNK, LANES), F32)],
        compiler_params=_cparams(1),
        name="hgrn",
    )(proj, proj, proj, proj, hgrn_lb, gnorm_w, tri3, w_up)


def _ssd_kernel(z_ref, xr_ref, bcr_ref, dt_ref, cwx_ref, cwbc_ref, cbx_ref, cbbc_ref, dtb_ref, alog_ref,
                dexp_ref, nw_ref, tri_ref, ecol_ref, e64_ref, shift_ref, wdn_ref, wout_ref,
                o_ref, wdn_bf_ref, wout_bf_ref,
                ht_ref, halo_x, halo_bc, cx_ref, cbc_ref, acol_all, ecol_all, ats_all, y_all, *, tile):
    c_len = CHUNK
    n_chunks = tile // c_len
    pair_w = 2 * SSM_P

    @pl.when(pl.program_id(0) == 0)
    def _():
        ht_ref[...] = jnp.zeros_like(ht_ref)
        halo_x[...] = jnp.zeros_like(halo_x)
        halo_bc[...] = jnp.zeros_like(halo_bc)

    wdn_bf_ref[...] = wdn_ref[...].astype(BF16)
    wout_bf_ref[...] = wout_ref[...].astype(BF16)

    slab = 512
    for raw_ref, halo, cw_ref, cb_ref, out in ((xr_ref, halo_x, cwx_ref, cbx_ref, cx_ref),
                                               (bcr_ref, halo_bc, cwbc_ref, cbbc_ref, cbc_ref)):
        for r in range(n_chunks):
            for s0 in range(0, D_SSM, slab):
                cols = slice(s0, s0 + slab)
                cur = raw_ref[r * c_len:(r + 1) * c_len, cols]
                prev = halo[:, cols] if r == 0 else raw_ref[(r - 1) * c_len:r * c_len, cols]
                sh = jnp.dot(shift_ref[...], jnp.concatenate([prev, cur], axis=0), preferred_element_type=F32)
                acc = cb_ref[:, cols] + cw_ref[SSM_CONV - 1:SSM_CONV, cols] * cur.astype(F32)
                for j in range(SSM_CONV - 1):
                    acc = acc + cw_ref[j:j + 1, cols] * sh[j * c_len:(j + 1) * c_len]
                out[r * c_len:(r + 1) * c_len, cols] = _silu(acc)
        halo[...] = raw_ref[tile - c_len:tile, :]

    lane = lax.broadcasted_iota(jnp.int32, (c_len, LANES), 1)
    row2 = lax.broadcasted_iota(jnp.int32, (c_len, c_len), 0)
    col2 = lax.broadcasted_iota(jnp.int32, (c_len, c_len), 1)
    causal = row2 >= col2
    first_head = lane < SSM_P
    head_lane = lax.broadcasted_iota(jnp.int32, (1, LANES), 1) < SSM_HEADS
    a_head = jnp.where(head_lane, -jnp.exp(alog_ref[...]) * LOG2E, 0.0)
    tri3 = tri_ref[...]

    def chunk(c, acol_s, ecol_s, ats_s, y_s):
        rows = slice(c * c_len, (c + 1) * c_len)
        dt = jax.nn.softplus(dt_ref[rows, :] + dtb_ref[...])
        acum = _cumsum_rows(tri3, dt * a_head)
        a_last = acum[c_len - 1:c_len, :]
        acol_s[...] = jnp.dot(jnp.concatenate(_split_bf16(acum, 3), axis=1), ecol_ref[...],
                              preferred_element_type=F32)
        ecol_s[...] = jnp.dot(jnp.concatenate(_split_bf16(jnp.exp2(acum), 2), axis=1),
                              ecol_ref[0:2 * LANES, :], preferred_element_type=F32)
        wdec = dt * jnp.exp2(a_last - acum)
        wexp = jnp.dot(jnp.concatenate(_split_bf16(wdec, 2), axis=1), e64_ref[...],
                       preferred_element_type=F32)
        ats_s[...] = (acum - jnp.log(dt) * LOG2E).T

        for g in range(SSM_GROUPS):
            bg = cbc_ref[rows, g * SSM_N:(g + 1) * SSM_N]
            cg = cbc_ref[rows, (SSM_GROUPS + g) * SSM_N:(SSM_GROUPS + g + 1) * SSM_N]
            cb = lax.dot_general(cg.astype(BF16), bg.astype(BF16), NT_DIMS, preferred_element_type=F32)
            bgt = bg.T.astype(BF16)
            for pp in range(2):
                pair = 2 * g + pp
                ha, hb = 2 * pair, 2 * pair + 1
                ps = slice(pair * pair_w, (pair + 1) * pair_w)
                xp = cx_ref[rows, ps]
                xpb = xp.astype(BF16)
                zero = jnp.zeros_like(xpb)
                y = None
                for hh, keep in ((ha, first_head), (hb, ~first_head)):
                    ex = acol_s[:, hh * LANES:(hh + 1) * LANES] - ats_s[hh:hh + 1, :]
                    m = (cb * jnp.exp2(jnp.where(causal, ex, -jnp.inf))).astype(BF16)
                    yh = jnp.dot(m, jnp.where(keep, xpb, zero), preferred_element_type=F32)
                    y = yh if y is None else y + yh
                ce = jnp.concatenate([(cg * ecol_s[:, ha * LANES:(ha + 1) * LANES]).astype(BF16),
                                      (cg * ecol_s[:, hb * LANES:(hb + 1) * LANES]).astype(BF16)], axis=1)
                ht = ht_ref[pair]
                y = y + jnp.dot(ce, ht.astype(BF16), preferred_element_type=F32)

                xw = (xp * wexp[:, ps]).astype(BF16)
                dca = ecol_s[c_len - 1:c_len, ha * LANES:(ha + 1) * LANES]
                dcb = ecol_s[c_len - 1:c_len, hb * LANES:(hb + 1) * LANES]
                ht_ref[pair, 0:SSM_N, :] = ht[0:SSM_N] * dca + jnp.dot(
                    bgt, jnp.where(first_head, xw, zero), preferred_element_type=F32)
                ht_ref[pair, SSM_N:2 * SSM_N, :] = ht[SSM_N:] * dcb + jnp.dot(
                    bgt, jnp.where(first_head, zero, xw), preferred_element_type=F32)

                y = y + dexp_ref[0:1, ps] * xp
                y_s[:, ps] = y * _silu(z_ref[rows, ps].astype(F32))

            gs = slice(g * 2 * pair_w, (g + 1) * 2 * pair_w)
            yg = y_s[:, gs]
            yn = yg * lax.rsqrt(jnp.mean(yg * yg, axis=-1, keepdims=True) + EPS)
            o_ref[rows, gs] = (yn * nw_ref[0:1, gs]).astype(BF16)

    for c in range(n_chunks):
        chunk(c, acol_all.at[c], ecol_all.at[c], ats_all.at[c], y_all.at[c])


def _ssd_call(proj, dt_raw, conv_w, conv_b, dt_bias, a_log, d_exp, norm_w, tri3, e_col, e_64, shift3,
              w_down, w_out, tile):
    s = proj.shape[0]
    n_steps = s // tile
    dn_col_blocks = 2 if n_steps % 2 == 0 else 1
    dn_rows = w_down.shape[0] // (n_steps // dn_col_blocks)
    dn_cols = w_down.shape[1] // dn_col_blocks
    out_rows = w_out.shape[0] // n_steps
    assert dn_rows % BF16_ROWS == 0 and out_rows % BF16_ROWS == 0

    def col(k):
        return pl.BlockSpec((tile, D_SSM), lambda i: (i, k))

    def full(shape):
        return pl.BlockSpec(shape, lambda i: (0, 0))

    dn_spec = pl.BlockSpec((dn_rows, dn_cols), lambda i: (i // dn_col_blocks, i % dn_col_blocks))
    out_spec = pl.BlockSpec((out_rows, w_out.shape[1]), lambda i: (i, 0))
    n_pairs = SSM_HEADS // 2
    return pl.pallas_call(
        functools.partial(_ssd_kernel, tile=tile),
        grid=(n_steps,),
        in_specs=[col(4), col(5), col(6),
                  pl.BlockSpec((tile, LANES), lambda i: (i, 0)),
                  pl.BlockSpec((SSM_CONV, D_SSM), lambda i: (0, 0)),
                  pl.BlockSpec((SSM_CONV, D_SSM), lambda i: (0, 1)),
                  pl.BlockSpec((1, D_SSM), lambda i: (0, 0)),
                  pl.BlockSpec((1, D_SSM), lambda i: (0, 1)),
                  full((1, LANES)), full((1, LANES)), full((1, D_SSM)), full((1, D_SSM)),
                  full((CHUNK, 3 * CHUNK)), full(e_col.shape), full(e_64.shape), full(shift3.shape),
                  dn_spec, out_spec],
        out_specs=[pl.BlockSpec((tile, D_SSM), lambda i: (i, 0)), dn_spec, out_spec],
        out_shape=[jax.ShapeDtypeStruct((s, D_SSM), BF16),
                   jax.ShapeDtypeStruct(w_down.shape, BF16),
                   jax.ShapeDtypeStruct(w_out.shape, BF16)],
        scratch_shapes=[pltpu.VMEM((n_pairs, 2 * SSM_N, LANES), F32),
                        pltpu.VMEM((CHUNK, D_SSM), BF16), pltpu.VMEM((CHUNK, D_SSM), BF16),
                        pltpu.VMEM((tile, D_SSM), F32), pltpu.VMEM((tile, D_SSM), F32),
                        pltpu.VMEM((tile // CHUNK, CHUNK, SSM_HEADS * LANES), F32),
                        pltpu.VMEM((tile // CHUNK, CHUNK, SSM_HEADS * LANES), F32),
                        pltpu.VMEM((tile // CHUNK, CHUNK, LANES), F32),
                        pltpu.VMEM((tile // CHUNK, CHUNK, D_SSM), F32)],
        compiler_params=_cparams(1),
        name="ssd",
    )(proj, proj, proj, dt_raw, conv_w, conv_w, conv_b, conv_b, dt_bias, a_log, d_exp, norm_w, tri3,
      e_col, e_64, shift3, w_down, w_out)


def _outproj_kernel(x_ref, oa_ref, ob_ref, wa_ref, wb_ref, g_ref, o_ref):
    mixed = jnp.dot(oa_ref[...], wa_ref[...], preferred_element_type=F32)
    mixed = mixed + jnp.dot(ob_ref[...], wb_ref[...], preferred_element_type=F32)
    o_ref[...] = x_ref[...] + g_ref[...] * mixed


def _outproj_call(x2, o_a, o_b, w_out_bf, mod, tm):
    s = x2.shape[0]
    return pl.pallas_call(
        _outproj_kernel,
        grid=(s // tm,),
        in_specs=[pl.BlockSpec((tm, D_MODEL), lambda i: (i, 0)),
                  pl.BlockSpec((tm, D_HGRN), lambda i: (i, 0)),
                  pl.BlockSpec((tm, D_SSM), lambda i: (i, 0)),
                  pl.BlockSpec((D_HGRN, D_MODEL), lambda i: (0, 0)),
                  pl.BlockSpec((D_SSM, D_MODEL), lambda i: (1, 0)),
                  pl.BlockSpec((1, D_MODEL), lambda i: (0, 2))],
        out_specs=pl.BlockSpec((tm, D_MODEL), lambda i: (i, 0)),
        out_shape=jax.ShapeDtypeStruct((s, D_MODEL), F32),
        compiler_params=_cparams(1),
        name="outproj",
    )(x2, o_a, o_b, w_out_bf, w_out_bf, mod)


def _ffn_kernel(x_ref, nw_ref, sc_ref, sh_ref, gate_ref, fnw_ref, wug_ref, wuv_ref, cwg_ref, cwv_ref,
                cbg_ref, cbv_ref, wd_ref, o_ref, h_ref, ug_ref, uv_ref, act0_ref, act1_ref, carry_ref,
                *, tm, n_slabs):
    i = pl.program_id(0)
    j = pl.program_id(1)
    rb = 128
    act_ref = (act0_ref, act1_ref)

    @pl.when(j == 0)
    def _():
        _norm_mod_rows(x_ref, nw_ref, sc_ref, sh_ref, h_ref, tm)
        o_ref[...] = jnp.zeros_like(o_ref)

    @pl.when((i == 0) & (j < n_slabs))
    def _():
        carry_ref[j] = jnp.zeros(carry_ref.shape[1:], F32)

    def up_conv(slot):
        h = h_ref[...]
        for u_ref, w_ref, which in ((ug_ref, wug_ref, 0), (uv_ref, wuv_ref, 1)):
            u_ref[0:HALO, :] = carry_ref[j, which]
            u_ref[HALO:HALO + tm, :] = jnp.dot(h, w_ref[...], preferred_element_type=F32)
            carry_ref[j, which] = u_ref[tm:tm + HALO, :]

        def conv(u_ref, cw_ref, cb_ref, r):
            acc = cb_ref[...] + cw_ref[FFN_CONV - 1:FFN_CONV, :] * u_ref[HALO + r * rb:HALO + (r + 1) * rb, :]
            for t in range(FFN_CONV - 1):
                off = HALO + r * rb - (FFN_CONV - 1) + t
                acc = acc + cw_ref[t:t + 1, :] * u_ref[off:off + rb, :]
            return acc

        for r in range(tm // rb):
            act_ref[slot][r * rb:(r + 1) * rb, :] = (
                _silu(conv(ug_ref, cwg_ref, cbg_ref, r)) * conv(uv_ref, cwv_ref, cbv_ref, r)).astype(BF16)

    def down(slot):
        o_ref[...] += jnp.dot(act_ref[slot][...], wd_ref[...], preferred_element_type=F32)

    @pl.when(j == 0)
    def _():
        up_conv(0)

    for parity in range(2):
        @pl.when((j > 0) & (j < n_slabs) & (j % 2 == parity))
        def _():
            down(1 - parity)
            up_conv(parity)

    @pl.when(j == n_slabs)
    def _():
        down((n_slabs - 1) % 2)
        gate = gate_ref[...]
        fnw = fnw_ref[...]

        def body(r, carry):
            rows = pl.ds(pl.multiple_of(r * rb, rb), rb)
            y = x_ref[rows, :] + gate * o_ref[rows, :]
            o_ref[rows, :] = y * lax.rsqrt(jnp.mean(y * y, axis=-1, keepdims=True) + EPS) * fnw
            return carry

        lax.fori_loop(0, tm // rb, body, 0)


def _ffn_call(x1, norm_w, mod, final_w, w_up_bf, conv_w, conv_b, w_down_bf, tm, tf):
    s = x1.shape[0]
    n_slabs = D_FF // tf

    def vec(k):
        return pl.BlockSpec((1, D_MODEL), lambda i, j: (0, k))

    def up(j):
        return jnp.minimum(j, n_slabs - 1)

    def dn(j):
        return jnp.maximum(j - 1, 0)

    return pl.pallas_call(
        functools.partial(_ffn_kernel, tm=tm, n_slabs=n_slabs),
        grid=(s // tm, n_slabs + 1),
        in_specs=[pl.BlockSpec((tm, D_MODEL), lambda i, j: (i, 0)),
                  pl.BlockSpec((1, D_MODEL), lambda i, j: (0, 0)),
                  vec(4), vec(3), vec(5),
                  pl.BlockSpec((1, D_MODEL), lambda i, j: (0, 0)),
                  pl.BlockSpec((D_MODEL, tf), lambda i, j: (0, up(j))),
                  pl.BlockSpec((D_MODEL, tf), lambda i, j: (0, n_slabs + up(j))),
                  pl.BlockSpec((FFN_CONV, tf), lambda i, j: (0, up(j))),
                  pl.BlockSpec((FFN_CONV, tf), lambda i, j: (0, n_slabs + up(j))),
                  pl.BlockSpec((1, tf), lambda i, j: (0, up(j))),
                  pl.BlockSpec((1, tf), lambda i, j: (0, n_slabs + up(j))),
                  pl.BlockSpec((tf, D_MODEL), lambda i, j: (dn(j), 0))],
        out_specs=pl.BlockSpec((tm, D_MODEL), lambda i, j: (i, 0)),
        out_shape=jax.ShapeDtypeStruct((s, D_MODEL), F32),
        scratch_shapes=[pltpu.VMEM((tm, D_MODEL), BF16),
                        pltpu.VMEM((tm + HALO, tf), F32), pltpu.VMEM((tm + HALO, tf), F32),
                        pltpu.VMEM((tm, tf), BF16), pltpu.VMEM((tm, tf), BF16),
                        pltpu.VMEM((n_slabs, 2, HALO, tf), F32)],
        compiler_params=_cparams(2),
        name="ffn",
    )(x1, norm_w, mod, mod, mod, final_w, w_up_bf, w_up_bf, conv_w, conv_w, conv_b, conv_b, w_down_bf)


def _tri3(n):
    tri = jnp.tril(jnp.ones((n, n), F32)).astype(BF16)
    return jnp.concatenate([tri, tri, tri], axis=1)


def _expand_matrix(n_terms, width):
    src = jnp.arange(n_terms * LANES) % LANES
    dst = jnp.arange(SSM_HEADS * width) // width
    return (src[:, None] == dst[None, :]).astype(BF16)


def _shift_matrix(n):
    t = jnp.arange((SSM_CONV - 1) * n)
    src = n + t % n - (SSM_CONV - 1) + t // n
    return (src[:, None] == jnp.arange(2 * n)[None, :]).astype(BF16)


def _pad_lanes(v):
    return jnp.pad(v.reshape(1, -1), ((0, 0), (0, LANES - v.shape[-1])))


def kernel(x, c, w_mod, b_mod, norm1_w, w_in, hgrn_lb, hgrn_gnorm_w, ssd_conv_w, ssd_conv_b, ssd_dt_bias,
           ssd_a_log, ssd_d, ssd_norm_w, w_out, norm2_w, ffn_w_up, ffn_conv_w, ffn_conv_b, ffn_w_down,
           final_norm_w):
    bsz, seq, _ = x.shape
    assert bsz == 1 and w_in.shape[0] == 1, "single batch element, single layer"
    assert seq % CHUNK == 0
    x2 = x.reshape(seq, D_MODEL)
    tm_in = min(seq, MATMUL_ROWS)
    tile_mix = min(seq, MIXER_ROWS)
    tile_ssd = min(seq, MIXER_ROWS)
    tm_out = min(seq, MIXER_ROWS)
    tm_ffn = min(seq, MATMUL_ROWS)

    mod = _mod_call(c.reshape(D_MODEL, 1), w_mod[0], b_mod)

    w_in_bf = w_in[0].astype(BF16)
    w_dt = jnp.pad(w_in_bf[:, D_MAIN:], ((0, 0), (0, LANES - SSM_HEADS)))
    proj, dt_raw = _inproj_call(x2, norm1_w, mod, w_in_bf, w_dt, tm_in, D_MAIN // INPROJ_COL_BLOCKS)

    tri3 = _tri3(CHUNK)
    o_a, w_up_bf = _hgrn_call(proj, hgrn_lb, hgrn_gnorm_w, tri3, ffn_w_up[0], tile_mix)
    o_b, w_down_bf, w_out_bf = _ssd_call(
        proj, dt_raw, ssd_conv_w[0], ssd_conv_b, _pad_lanes(ssd_dt_bias[0]), _pad_lanes(ssd_a_log[0]),
        jnp.repeat(ssd_d[0], SSM_P).reshape(1, D_SSM), ssd_norm_w, tri3, _expand_matrix(3, LANES),
        _expand_matrix(2, SSM_P), _shift_matrix(CHUNK), ffn_w_down[0], w_out[0], tile_ssd)

    x1 = _outproj_call(x2, o_a, o_b, w_out_bf, mod, tm_out)

    out = _ffn_call(x1, norm2_w, mod, final_norm_w.reshape(1, D_MODEL), w_up_bf, ffn_conv_w[0], ffn_conv_b,
                    w_down_bf, tm_ffn, FF_SLAB)
    return out.reshape(bsz, seq, D_MODEL)
```

```python
import functools
import math

import jax
import jax.numpy as jnp
from jax import lax
from jax.experimental import pallas as pl
from jax.experimental.pallas import tpu as pltpu

F32 = jnp.float32
BF16 = jnp.bfloat16

D_MODEL = 2048
D_HGRN = 1024
HGRN_HEADS = 8
HEAD_K = 128
D_SSM = 1024
SSM_HEADS = 16
SSM_P = 64
SSM_GROUPS = 4
SSM_N = 128
SSM_CONV = 4
D_MAIN = 4 * D_HGRN + D_SSM + (D_SSM + 2 * SSM_GROUPS * SSM_N)
D_FF = 5632
FFN_CONV = 3
EPS = 1e-6
LOG2E = math.log2(math.e)

LANES = 128
SUBLANES = 8
BF16_ROWS = 16
CHUNK = 128
CUMSUM_TERMS = 2
FAST_LEVELS = 6
FAST_MAX_LOG2 = 100.0
HALO = SUBLANES
VMEM_LIMIT = 60 * 1024 * 1024

MATMUL_ROWS = 1024
MIXER_ROWS = 512
SSD_ROWS = 1024
INPROJ_COL_BLOCKS = 4
FF_SLAB = 512

NT_DIMS = (((1,), (1,)), ((), ()))


def _cparams(n_axes):
    return pltpu.CompilerParams(dimension_semantics=("arbitrary",) * n_axes, vmem_limit_bytes=VMEM_LIMIT)


def _sigmoid(x):
    return 1.0 / (1.0 + jnp.exp(-x))


def _silu(x):
    return x * _sigmoid(x)


def _split_bf16(x, n_terms):
    terms = []
    r = x
    for t in range(n_terms):
        p = r.astype(BF16)
        terms.append(p)
        if t + 1 < n_terms:
            r = r - p.astype(F32)
    return terms


def _cumsum_rows(tri3, x):
    c_len = x.shape[0]
    return jnp.dot(tri3[:, :CUMSUM_TERMS * c_len], jnp.concatenate(_split_bf16(x, CUMSUM_TERMS), axis=0),
                   preferred_element_type=F32)


def _mod_kernel(c_ref, w_ref, b_ref, o_ref):
    s = _silu(c_ref[...])
    o_ref[...] = jnp.sum(w_ref[...] * s, axis=0, keepdims=True) + b_ref[...]


def _mod_call(c_col, w_mod, b_mod):
    d, n = w_mod.shape
    tn = 1024
    return pl.pallas_call(
        _mod_kernel,
        grid=(n // tn,),
        in_specs=[pl.BlockSpec((d, 1), lambda j: (0, 0)),
                  pl.BlockSpec((d, tn), lambda j: (0, j)),
                  pl.BlockSpec((1, tn), lambda j: (0, j))],
        out_specs=pl.BlockSpec((1, tn), lambda j: (0, j)),
        out_shape=jax.ShapeDtypeStruct((1, n), F32),
        compiler_params=_cparams(1),
        name="mod",
    )(c_col, w_mod, b_mod)


def _norm_mod_rows(x_ref, nw_ref, sc_ref, sh_ref, h_ref, n_rows, row_block=128):
    w = nw_ref[...] * (1.0 + sc_ref[...])
    sh = sh_ref[...]

    def body(r, carry):
        rows = pl.ds(pl.multiple_of(r * row_block, row_block), row_block)
        x = x_ref[rows, :]
        y = x * lax.rsqrt(jnp.mean(x * x, axis=-1, keepdims=True) + EPS)
        h_ref[rows, :] = (y * w + sh).astype(BF16)
        return carry

    lax.fori_loop(0, n_rows // row_block, body, 0)


def _inproj_kernel(x_ref, nw_ref, sc_ref, sh_ref, w_ref, wdt_ref, o_ref, dt_ref, h_ref, *, tm):
    @pl.when(pl.program_id(1) == 0)
    def _():
        _norm_mod_rows(x_ref, nw_ref, sc_ref, sh_ref, h_ref, tm)
        dt_ref[...] = jnp.dot(h_ref[...], wdt_ref[...], preferred_element_type=F32)

    o_ref[...] = jnp.dot(h_ref[...], w_ref[...], preferred_element_type=F32).astype(BF16)


def _inproj_call(x2, norm_w, mod, w_bf, w_dt, tm, tn):
    s = x2.shape[0]
    return pl.pallas_call(
        functools.partial(_inproj_kernel, tm=tm),
        grid=(s // tm, D_MAIN // tn),
        in_specs=[pl.BlockSpec((tm, D_MODEL), lambda i, j: (i, 0)),
                  pl.BlockSpec((1, D_MODEL), lambda i, j: (0, 0)),
                  pl.BlockSpec((1, D_MODEL), lambda i, j: (0, 1)),
                  pl.BlockSpec((1, D_MODEL), lambda i, j: (0, 0)),
                  pl.BlockSpec((D_MODEL, tn), lambda i, j: (0, j)),
                  pl.BlockSpec((D_MODEL, LANES), lambda i, j: (0, 0))],
        out_specs=[pl.BlockSpec((tm, tn), lambda i, j: (i, j)),
                   pl.BlockSpec((tm, LANES), lambda i, j: (i, 0))],
        out_shape=[jax.ShapeDtypeStruct((s, D_MAIN), BF16),
                   jax.ShapeDtypeStruct((s, LANES), F32)],
        scratch_shapes=[pltpu.VMEM((tm, D_MODEL), BF16)],
        compiler_params=_cparams(2),
        name="inproj",
    )(x2, norm_w, mod, mod, w_bf, w_dt)


def _row_bcast(b_ref, bh, row, n_rows):
    return jnp.broadcast_to(b_ref[bh, row:row + 1, :], (n_rows, LANES))


def _hgrn_kernel(q_ref, f_ref, v_ref, g_ref, lb_ref, gnw_ref, tri_ref, wup_ref, o_ref, wup_bf_ref,
                 st_ref, b_ref, fg_ref, *, n_chunks):
    c_len = CHUNK
    n_levels = c_len.bit_length() - 1

    @pl.when(pl.program_id(0) == 0)
    def _():
        st_ref[...] = jnp.zeros_like(st_ref)

    wup_bf_ref[...] = wup_ref[...].astype(BF16)

    row2 = lax.broadcasted_iota(jnp.int32, (c_len, c_len), 0)
    col2 = lax.broadcasted_iota(jnp.int32, (c_len, c_len), 1)
    xor2 = row2 ^ col2
    level = jnp.where(row2 == col2, n_levels, -1)
    for lvl in range(n_levels):
        level = jnp.where(((xor2 >> lvl) == 1) & (row2 > col2), lvl, level)
    in_block = (level == n_levels) | ((level >= 0) & (level < FAST_LEVELS))
    rowk = lax.broadcasted_iota(jnp.int32, (c_len, LANES), 0)
    sub8 = lax.broadcasted_iota(jnp.int32, (SUBLANES, LANES), 0)
    tri3 = tri_ref[...]

    def block_anchor(bh, blk):
        size = 1 << FAST_LEVELS
        return _row_bcast(b_ref, bh, blk * size + size // 2 - 1, size)

    def gates(c):
        rows = slice(c * c_len, (c + 1) * c_len)
        worst = jnp.zeros((1 << FAST_LEVELS, LANES), F32)
        for h in range(HGRN_HEADS):
            cs = slice(h * HEAD_K, (h + 1) * HEAD_K)
            bh = c * HGRN_HEADS + h
            a0 = lb_ref[0:1, cs]
            a1 = lb_ref[1:2, cs]
            amax = jnp.maximum(a0, a1)
            e0 = jnp.exp(a0 - amax)
            lb = e0 / (e0 + jnp.exp(a1 - amax))
            fg = lb + (1.0 - lb) * _sigmoid(f_ref[rows, cs].astype(F32))
            fg_ref[bh] = fg
            b = _cumsum_rows(tri3, jnp.log(fg) * LOG2E)
            b_ref[bh] = b
            for blk in range(c_len >> FAST_LEVELS):
                span = slice(blk << FAST_LEVELS, (blk + 1) << FAST_LEVELS)
                worst = jnp.maximum(worst, jnp.abs(b[span] - block_anchor(bh, blk)))
        return jnp.max(worst) <= FAST_MAX_LOG2

    def heads(c, fast):
        rows = slice(c * c_len, (c + 1) * c_len)
        for h in range(HGRN_HEADS):
            cs = slice(h * HEAD_K, (h + 1) * HEAD_K)
            bh = c * HGRN_HEADS + h
            qs = _silu(q_ref[rows, cs].astype(F32))
            vb = v_ref[rows, cs]
            fg = fg_ref[bh]
            k = 1.0 - fg
            b = b_ref[bh]

            if fast:
                d = b - jnp.concatenate([block_anchor(bh, blk) for blk in range(c_len >> FAST_LEVELS)], axis=0)
                att = jnp.where(in_block,
                                lax.dot_general((qs * jnp.exp2(d)).astype(BF16), (k * jnp.exp2(-d)).astype(BF16),
                                                NT_DIMS, preferred_element_type=F32), 0.0)
            else:
                att = jnp.where(level == n_levels,
                                lax.dot_general(qs.astype(BF16), k.astype(BF16), NT_DIMS,
                                                preferred_element_type=F32), 0.0)
            for lvl in range(FAST_LEVELS if fast else 0, n_levels):
                m = 1 << lvl
                if m >= SUBLANES:
                    parts = []
                    for blk in range(c_len // (2 * m)):
                        lo = slice(blk * 2 * m, blk * 2 * m + m)
                        up = slice(blk * 2 * m + m, (blk + 1) * 2 * m)
                        bc = _row_bcast(b_ref, bh, blk * 2 * m + m - 1, m)
                        parts.append(k[lo] * jnp.exp2(bc - b[lo]))
                        parts.append(qs[up] * jnp.exp2(b[up] - bc))
                    xl = jnp.concatenate(parts, axis=0)
                else:
                    upper = ((rowk >> lvl) & 1) == 1
                    if lvl == 0:
                        xl = jnp.where(upper, qs * fg, k)
                    else:
                        tiles = []
                        for t8 in range(c_len // SUBLANES):
                            if lvl == 2:
                                tiles.append(_row_bcast(b_ref, bh, t8 * 8 + 3, SUBLANES))
                            else:
                                tiles.append(jnp.where(sub8 < 4, _row_bcast(b_ref, bh, t8 * 8 + 1, SUBLANES),
                                                       _row_bcast(b_ref, bh, t8 * 8 + 5, SUBLANES)))
                        d = b - jnp.concatenate(tiles, axis=0)
                        xl = jnp.where(upper, qs, k) * jnp.exp2(jnp.where(upper, d, -d))
                xb = xl.astype(BF16)
                att = jnp.where(level == lvl,
                                lax.dot_general(xb, xb, NT_DIMS, preferred_element_type=F32), att)

            st = st_ref[h]
            qc = (qs * jnp.exp2(b)).astype(BF16)
            o = jnp.dot(att.astype(BF16), vb, preferred_element_type=F32)
            o = o + lax.dot_general(qc, st.astype(BF16), NT_DIMS, preferred_element_type=F32)
            b_last = b_ref[bh, c_len - 1:c_len, :]
            ke = (k * jnp.exp2(b_last - b)).astype(BF16)
            vt = vb.T
            st_ref[h] = st * jnp.exp2(b_last) + jnp.dot(vt, ke, preferred_element_type=F32)

            on = o * lax.rsqrt(jnp.mean(o * o, axis=-1, keepdims=True) + EPS)
            o_ref[rows, cs] = (on * gnw_ref[0:1, cs] * _silu(g_ref[rows, cs].astype(F32))).astype(BF16)

    fast_ok = gates(0)
    for c in range(1, n_chunks):
        fast_ok = fast_ok & gates(c)

    @pl.when(fast_ok)
    def _():
        for c in range(n_chunks):
            heads(c, True)

    @pl.when(jnp.logical_not(fast_ok))
    def _():
        for c in range(n_chunks):
            heads(c, False)


def _hgrn_call(proj, hgrn_lb, gnorm_w, tri3, w_up, tile):
    s = proj.shape[0]
    n_steps = s // tile
    up_rows, up_cols = w_up.shape
    assert up_rows % (n_steps * BF16_ROWS) == 0

    def col(k):
        return pl.BlockSpec((tile, D_HGRN), lambda i: (i, k))

    return pl.pallas_call(
        functools.partial(_hgrn_kernel, n_chunks=tile // CHUNK),
        grid=(n_steps,),
        in_specs=[col(0), col(1), col(2), col(3),
                  pl.BlockSpec((2, D_HGRN), lambda i: (0, 0)),
                  pl.BlockSpec((1, D_HGRN), lambda i: (0, 0)),
                  pl.BlockSpec((CHUNK, 3 * CHUNK), lambda i: (0, 0)),
                  pl.BlockSpec((up_rows // n_steps, up_cols), lambda i: (i, 0))],
        out_specs=[pl.BlockSpec((tile, D_HGRN), lambda i: (i, 0)),
                   pl.BlockSpec((up_rows // n_steps, up_cols), lambda i: (i, 0))],
        out_shape=[jax.ShapeDtypeStruct((s, D_HGRN), BF16),
                   jax.ShapeDtypeStruct(w_up.shape, BF16)],
        scratch_shapes=[pltpu.VMEM((HGRN_HEADS, HEAD_K, HEAD_K), F32),
                        pltpu.VMEM((tile // CHUNK * HGRN_HEADS, CHUNK, LANES), F32),
                        pltpu.VMEM((tile // CHUNK * HGRN_HEADS, CHUNK, LANES), F32)],
        compiler_params=_cparams(1),
        name="hgrn",
    )(proj, proj, proj, proj, hgrn_lb, gnorm_w, tri3, w_up)


def _ssd_kernel(z_ref, xr_ref, bcr_ref, dt_ref, cwx_ref, cwbc_ref, cbx_ref, cbbc_ref, dtb_ref, alog_ref,
                dexp_ref, nw_ref, tri_ref, ecol_ref, e64_ref, shift_ref, wdn_ref, wout_ref,
                o_ref, wdn_bf_ref, wout_bf_ref,
                ht_ref, halo_x, halo_bc, cx_ref, cbc_ref, acol_all, ecol_all, ats_all, y_all, *, tile):
    c_len = CHUNK
    n_chunks = tile // c_len
    pair_w = 2 * SSM_P

    @pl.when(pl.program_id(0) == 0)
    def _():
        ht_ref[...] = jnp.zeros_like(ht_ref)
        halo_x[...] = jnp.zeros_like(halo_x)
        halo_bc[...] = jnp.zeros_like(halo_bc)

    wdn_bf_ref[...] = wdn_ref[...].astype(BF16)
    wout_bf_ref[...] = wout_ref[...].astype(BF16)

    slab = 512
    for raw_ref, halo, cw_ref, cb_ref, out in ((xr_ref, halo_x, cwx_ref, cbx_ref, cx_ref),
                                               (bcr_ref, halo_bc, cwbc_ref, cbbc_ref, cbc_ref)):
        for r in range(n_chunks):
            for s0 in range(0, D_SSM, slab):
                cols = slice(s0, s0 + slab)
                cur = raw_ref[r * c_len:(r + 1) * c_len, cols]
                prev = halo[:, cols] if r == 0 else raw_ref[(r - 1) * c_len:r * c_len, cols]
                sh = jnp.dot(shift_ref[...], jnp.concatenate([prev, cur], axis=0), preferred_element_type=F32)
                acc = cb_ref[:, cols] + cw_ref[SSM_CONV - 1:SSM_CONV, cols] * cur.astype(F32)
                for j in range(SSM_CONV - 1):
                    acc = acc + cw_ref[j:j + 1, cols] * sh[j * c_len:(j + 1) * c_len]
                out[r * c_len:(r + 1) * c_len, cols] = _silu(acc)
        halo[...] = raw_ref[tile - c_len:tile, :]

    lane = lax.broadcasted_iota(jnp.int32, (c_len, LANES), 1)
    row2 = lax.broadcasted_iota(jnp.int32, (c_len, c_len), 0)
    col2 = lax.broadcasted_iota(jnp.int32, (c_len, c_len), 1)
    causal = row2 >= col2
    first_head = lane < SSM_P
    head_lane = lax.broadcasted_iota(jnp.int32, (1, LANES), 1) < SSM_HEADS
    a_head = jnp.where(head_lane, -jnp.exp(alog_ref[...]) * LOG2E, 0.0)
    tri3 = tri_ref[...]

    def chunk(c, acol_s, ecol_s, ats_s, y_s):
        rows = slice(c * c_len, (c + 1) * c_len)
        dt = jax.nn.softplus(dt_ref[rows, :] + dtb_ref[...])
        acum = _cumsum_rows(tri3, dt * a_head)
        a_last = acum[c_len - 1:c_len, :]
        acol_s[...] = jnp.dot(jnp.concatenate(_split_bf16(acum, 3), axis=1), ecol_ref[...],
                              preferred_element_type=F32)
        ecol_s[...] = jnp.dot(jnp.concatenate(_split_bf16(jnp.exp2(acum), 2), axis=1),
                              ecol_ref[0:2 * LANES, :], preferred_element_type=F32)
        wdec = dt * jnp.exp2(a_last - acum)
        wexp = jnp.dot(jnp.concatenate(_split_bf16(wdec, 2), axis=1), e64_ref[...],
                       preferred_element_type=F32)
        ats_s[...] = (acum - jnp.log(dt) * LOG2E).T

        for g in range(SSM_GROUPS):
            bg = cbc_ref[rows, g * SSM_N:(g + 1) * SSM_N]
            cg = cbc_ref[rows, (SSM_GROUPS + g) * SSM_N:(SSM_GROUPS + g + 1) * SSM_N]
            cb = lax.dot_general(cg.astype(BF16), bg.astype(BF16), NT_DIMS, preferred_element_type=F32)
            bgt = bg.T.astype(BF16)
            for pp in range(2):
                pair = 2 * g + pp
                ha, hb = 2 * pair, 2 * pair + 1
                ps = slice(pair * pair_w, (pair + 1) * pair_w)
                xp = cx_ref[rows, ps]
                xpb = xp.astype(BF16)
                zero = jnp.zeros_like(xpb)
                y = None
                for hh, keep in ((ha, first_head), (hb, ~first_head)):
                    ex = acol_s[:, hh * LANES:(hh + 1) * LANES] - ats_s[hh:hh + 1, :]
                    m = (cb * jnp.exp2(jnp.where(causal, ex, -jnp.inf))).astype(BF16)
                    yh = jnp.dot(m, jnp.where(keep, xpb, zero), preferred_element_type=F32)
                    y = yh if y is None else y + yh
                ce = jnp.concatenate([(cg * ecol_s[:, ha * LANES:(ha + 1) * LANES]).astype(BF16),
                                      (cg * ecol_s[:, hb * LANES:(hb + 1) * LANES]).astype(BF16)], axis=1)
                ht = ht_ref[pair]
                y = y + jnp.dot(ce, ht.astype(BF16), preferred_element_type=F32)

                xw = (xp * wexp[:, ps]).astype(BF16)
                dca = ecol_s[c_len - 1:c_len, ha * LANES:(ha + 1) * LANES]
                dcb = ecol_s[c_len - 1:c_len, hb * LANES:(hb + 1) * LANES]
                ht_ref[pair, 0:SSM_N, :] = ht[0:SSM_N] * dca + jnp.dot(
                    bgt, jnp.where(first_head, xw, zero), preferred_element_type=F32)
                ht_ref[pair, SSM_N:2 * SSM_N, :] = ht[SSM_N:] * dcb + jnp.dot(
                    bgt, jnp.where(first_head, zero, xw), preferred_element_type=F32)

                y = y + dexp_ref[0:1, ps] * xp
                y_s[:, ps] = y * _silu(z_ref[rows, ps].astype(F32))

            gs = slice(g * 2 * pair_w, (g + 1) * 2 * pair_w)
            yg = y_s[:, gs]
            yn = yg * lax.rsqrt(jnp.mean(yg * yg, axis=-1, keepdims=True) + EPS)
            o_ref[rows, gs] = (yn * nw_ref[0:1, gs]).astype(BF16)

    for c in range(n_chunks):
        chunk(c, acol_all.at[c], ecol_all.at[c], ats_all.at[c], y_all.at[c])


def _ssd_call(proj, dt_raw, conv_w, conv_b, dt_bias, a_log, d_exp, norm_w, tri3, e_col, e_64, shift3,
              w_down, w_out, tile):
    s = proj.shape[0]
    n_steps = s // tile
    dn_col_blocks = 2 if n_steps % 2 == 0 else 1
    dn_rows = w_down.shape[0] // (n_steps // dn_col_blocks)
    dn_cols = w_down.shape[1] // dn_col_blocks
    out_rows = w_out.shape[0] // n_steps
    assert dn_rows % BF16_ROWS == 0 and out_rows % BF16_ROWS == 0

    def col(k):
        return pl.BlockSpec((tile, D_SSM), lambda i: (i, k))

    def full(shape):
        return pl.BlockSpec(shape, lambda i: (0, 0))

    dn_spec = pl.BlockSpec((dn_rows, dn_cols), lambda i: (i // dn_col_blocks, i % dn_col_blocks))
    out_spec = pl.BlockSpec((out_rows, w_out.shape[1]), lambda i: (i, 0))
    n_pairs = SSM_HEADS // 2
    return pl.pallas_call(
        functools.partial(_ssd_kernel, tile=tile),
        grid=(n_steps,),
        in_specs=[col(4), col(5), col(6),
                  pl.BlockSpec((tile, LANES), lambda i: (i, 0)),
                  pl.BlockSpec((SSM_CONV, D_SSM), lambda i: (0, 0)),
                  pl.BlockSpec((SSM_CONV, D_SSM), lambda i: (0, 1)),
                  pl.BlockSpec((1, D_SSM), lambda i: (0, 0)),
                  pl.BlockSpec((1, D_SSM), lambda i: (0, 1)),
                  full((1, LANES)), full((1, LANES)), full((1, D_SSM)), full((1, D_SSM)),
                  full((CHUNK, 3 * CHUNK)), full(e_col.shape), full(e_64.shape), full(shift3.shape),
                  dn_spec, out_spec],
        out_specs=[pl.BlockSpec((tile, D_SSM), lambda i: (i, 0)), dn_spec, out_spec],
        out_shape=[jax.ShapeDtypeStruct((s, D_SSM), BF16),
                   jax.ShapeDtypeStruct(w_down.shape, BF16),
                   jax.ShapeDtypeStruct(w_out.shape, BF16)],
        scratch_shapes=[pltpu.VMEM((n_pairs, 2 * SSM_N, LANES), F32),
                        pltpu.VMEM((CHUNK, D_SSM), BF16), pltpu.VMEM((CHUNK, D_SSM), BF16),
                        pltpu.VMEM((tile, D_SSM), F32), pltpu.VMEM((tile, D_SSM), F32),
                        pltpu.VMEM((tile // CHUNK, CHUNK, SSM_HEADS * LANES), F32),
                        pltpu.VMEM((tile // CHUNK, CHUNK, SSM_HEADS * LANES), F32),
                        pltpu.VMEM((tile // CHUNK, CHUNK, LANES), F32),
                        pltpu.VMEM((tile // CHUNK, CHUNK, D_SSM), F32)],
        compiler_params=_cparams(1),
        name="ssd",
    )(proj, proj, proj, dt_raw, conv_w, conv_w, conv_b, conv_b, dt_bias, a_log, d_exp, norm_w, tri3,
      e_col, e_64, shift3, w_down, w_out)


def _outproj_kernel(x_ref, oa_ref, ob_ref, wa_ref, wb_ref, g_ref, o_ref):
    mixed = jnp.dot(oa_ref[...], wa_ref[...], preferred_element_type=F32)
    mixed = mixed + jnp.dot(ob_ref[...], wb_ref[...], preferred_element_type=F32)
    o_ref[...] = x_ref[...] + g_ref[...] * mixed


def _outproj_call(x2, o_a, o_b, w_out_bf, mod, tm):
    s = x2.shape[0]
    return pl.pallas_call(
        _outproj_kernel,
        grid=(s // tm,),
        in_specs=[pl.BlockSpec((tm, D_MODEL), lambda i: (i, 0)),
                  pl.BlockSpec((tm, D_HGRN), lambda i: (i, 0)),
                  pl.BlockSpec((tm, D_SSM), lambda i: (i, 0)),
                  pl.BlockSpec((D_HGRN, D_MODEL), lambda i: (0, 0)),
                  pl.BlockSpec((D_SSM, D_MODEL), lambda i: (1, 0)),
                  pl.BlockSpec((1, D_MODEL), lambda i: (0, 2))],
        out_specs=pl.BlockSpec((tm, D_MODEL), lambda i: (i, 0)),
        out_shape=jax.ShapeDtypeStruct((s, D_MODEL), F32),
        compiler_params=_cparams(1),
        name="outproj",
    )(x2, o_a, o_b, w_out_bf, w_out_bf, mod)


def _ffn_kernel(x_ref, nw_ref, sc_ref, sh_ref, gate_ref, fnw_ref, wug_ref, wuv_ref, cwg_ref, cwv_ref,
                cbg_ref, cbv_ref, wd_ref, o_ref, h_ref, ug_ref, uv_ref, act0_ref, act1_ref, carry_ref,
                *, tm, n_slabs):
    i = pl.program_id(0)
    j = pl.program_id(1)
    rb = 128
    act_ref = (act0_ref, act1_ref)

    @pl.when(j == 0)
    def _():
        _norm_mod_rows(x_ref, nw_ref, sc_ref, sh_ref, h_ref, tm)
        o_ref[...] = jnp.zeros_like(o_ref)

    @pl.when((i == 0) & (j < n_slabs))
    def _():
        carry_ref[j] = jnp.zeros(carry_ref.shape[1:], F32)

    def up_conv(slot):
        h = h_ref[...]
        for u_ref, w_ref, which in ((ug_ref, wug_ref, 0), (uv_ref, wuv_ref, 1)):
            u_ref[0:HALO, :] = carry_ref[j, which]
            u_ref[HALO:HALO + tm, :] = jnp.dot(h, w_ref[...], preferred_element_type=F32)
            carry_ref[j, which] = u_ref[tm:tm + HALO, :]

        def conv(u_ref, cw_ref, cb_ref, r):
            acc = cb_ref[...] + cw_ref[FFN_CONV - 1:FFN_CONV, :] * u_ref[HALO + r * rb:HALO + (r + 1) * rb, :]
            for t in range(FFN_CONV - 1):
                off = HALO + r * rb - (FFN_CONV - 1) + t
                acc = acc + cw_ref[t:t + 1, :] * u_ref[off:off + rb, :]
            return acc

        for r in range(tm // rb):
            act_ref[slot][r * rb:(r + 1) * rb, :] = (
                _silu(conv(ug_ref, cwg_ref, cbg_ref, r)) * conv(uv_ref, cwv_ref, cbv_ref, r)).astype(BF16)

    def down(slot):
        o_ref[...] += jnp.dot(act_ref[slot][...], wd_ref[...], preferred_element_type=F32)

    @pl.when(j == 0)
    def _():
        up_conv(0)

    for parity in range(2):
        @pl.when((j > 0) & (j < n_slabs) & (j % 2 == parity))
        def _():
            down(1 - parity)
            up_conv(parity)

    @pl.when(j == n_slabs)
    def _():
        down((n_slabs - 1) % 2)
        gate = gate_ref[...]
        fnw = fnw_ref[...]

        def body(r, carry):
            rows = pl.ds(pl.multiple_of(r * rb, rb), rb)
            y = x_ref[rows, :] + gate * o_ref[rows, :]
            o_ref[rows, :] = y * lax.rsqrt(jnp.mean(y * y, axis=-1, keepdims=True) + EPS) * fnw
            return carry

        lax.fori_loop(0, tm // rb, body, 0)


def _ffn_call(x1, norm_w, mod, final_w, w_up_bf, conv_w, conv_b, w_down_bf, tm, tf):
    s = x1.shape[0]
    n_slabs = D_FF // tf

    def vec(k):
        return pl.BlockSpec((1, D_MODEL), lambda i, j: (0, k))

    def up(j):
        return jnp.minimum(j, n_slabs - 1)

    def dn(j):
        return jnp.maximum(j - 1, 0)

    return pl.pallas_call(
        functools.partial(_ffn_kernel, tm=tm, n_slabs=n_slabs),
        grid=(s // tm, n_slabs + 1),
        in_specs=[pl.BlockSpec((tm, D_MODEL), lambda i, j: (i, 0)),
                  pl.BlockSpec((1, D_MODEL), lambda i, j: (0, 0)),
                  vec(4), vec(3), vec(5),
                  pl.BlockSpec((1, D_MODEL), lambda i, j: (0, 0)),
                  pl.BlockSpec((D_MODEL, tf), lambda i, j: (0, up(j))),
                  pl.BlockSpec((D_MODEL, tf), lambda i, j: (0, n_slabs + up(j))),
                  pl.BlockSpec((FFN_CONV, tf), lambda i, j: (0, up(j))),
                  pl.BlockSpec((FFN_CONV, tf), lambda i, j: (0, n_slabs + up(j))),
                  pl.BlockSpec((1, tf), lambda i, j: (0, up(j))),
                  pl.BlockSpec((1, tf), lambda i, j: (0, n_slabs + up(j))),
                  pl.BlockSpec((tf, D_MODEL), lambda i, j: (dn(j), 0))],
        out_specs=pl.BlockSpec((tm, D_MODEL), lambda i, j: (i, 0)),
        out_shape=jax.ShapeDtypeStruct((s, D_MODEL), F32),
        scratch_shapes=[pltpu.VMEM((tm, D_MODEL), BF16),
                        pltpu.VMEM((tm + HALO, tf), F32), pltpu.VMEM((tm + HALO, tf), F32),
                        pltpu.VMEM((tm, tf), BF16), pltpu.VMEM((tm, tf), BF16),
                        pltpu.VMEM((n_slabs, 2, HALO, tf), F32)],
        compiler_params=_cparams(2),
        name="ffn",
    )(x1, norm_w, mod, mod, mod, final_w, w_up_bf, w_up_bf, conv_w, conv_w, conv_b, conv_b, w_down_bf)


def _tri3(n):
    tri = jnp.tril(jnp.ones((n, n), F32)).astype(BF16)
    return jnp.concatenate([tri, tri, tri], axis=1)


def _expand_matrix(n_terms, width):
    src = jnp.arange(n_terms * LANES) % LANES
    dst = jnp.arange(SSM_HEADS * width) // width
    return (src[:, None] == dst[None, :]).astype(BF16)


def _shift_matrix(n):
    t = jnp.arange((SSM_CONV - 1) * n)
    src = n + t % n - (SSM_CONV - 1) + t // n
    return (src[:, None] == jnp.arange(2 * n)[None, :]).astype(BF16)


def _pad_lanes(v):
    return jnp.pad(v.reshape(1, -1), ((0, 0), (0, LANES - v.shape[-1])))


def kernel(x, c, w_mod, b_mod, norm1_w, w_in, hgrn_lb, hgrn_gnorm_w, ssd_conv_w, ssd_conv_b, ssd_dt_bias,
           ssd_a_log, ssd_d, ssd_norm_w, w_out, norm2_w, ffn_w_up, ffn_conv_w, ffn_conv_b, ffn_w_down,
           final_norm_w):
    bsz, seq, _ = x.shape
    assert bsz == 1 and w_in.shape[0] == 1, "single batch element, single layer"
    assert seq % CHUNK == 0
    x2 = x.reshape(seq, D_MODEL)
    tm_in = min(seq, MATMUL_ROWS)
    tile_mix = min(seq, MIXER_ROWS)
    tile_ssd = min(seq, SSD_ROWS)
    tm_out = min(seq, MIXER_ROWS)
    tm_ffn = min(seq, MATMUL_ROWS)

    mod = _mod_call(c.reshape(D_MODEL, 1), w_mod[0], b_mod)

    w_in_bf = w_in[0].astype(BF16)
    w_dt = jnp.pad(w_in_bf[:, D_MAIN:], ((0, 0), (0, LANES - SSM_HEADS)))
    proj, dt_raw = _inproj_call(x2, norm1_w, mod, w_in_bf, w_dt, tm_in, D_MAIN // INPROJ_COL_BLOCKS)

    tri3 = _tri3(CHUNK)
    o_a, w_up_bf = _hgrn_call(proj, hgrn_lb, hgrn_gnorm_w, tri3, ffn_w_up[0], tile_mix)
    o_b, w_down_bf, w_out_bf = _ssd_call(
        proj, dt_raw, ssd_conv_w[0], ssd_conv_b, _pad_lanes(ssd_dt_bias[0]), _pad_lanes(ssd_a_log[0]),
        jnp.repeat(ssd_d[0], SSM_P).reshape(1, D_SSM), ssd_norm_w, tri3, _expand_matrix(3, LANES),
        _expand_matrix(2, SSM_P), _shift_matrix(CHUNK), ffn_w_down[0], w_out[0], tile_ssd)

    x1 = _outproj_call(x2, o_a, o_b, w_out_bf, mod, tm_out)

    out = _ffn_call(x1, norm2_w, mod, final_norm_w.reshape(1, D_MODEL), w_up_bf, ffn_conv_w[0], ffn_conv_b,
                    w_down_bf, tm_ffn, FF_SLAB)
    return out.reshape(bsz, seq, D_MODEL)
```
